```python
import jax, jax.numpy as jnp
from jax import lax
import numpy as np

D_MODEL = 2048
BATCH = 2
SEQ = 4096
DEPTH = 4

CTX_LEN = 256
GRID_W = 64
N_MIXERS = 3
N_MLA = len(range(0, DEPTH, N_MIXERS))
N_CONV = len(range(1, DEPTH, N_MIXERS))
N_GMLP = len(range(2, DEPTH, N_MIXERS))
N_HEADS = 16
QK_NOPE_DIM = 128
QK_ROPE_DIM = 64
QK_DIM = QK_NOPE_DIM + QK_ROPE_DIM
V_HEAD_DIM = 128
Q_LORA_RANK = 768
KV_LORA_RANK = 512
ROPE_THETA = 10000.0
ATTN_SCALE = QK_DIM ** -0.5
Q_BLOCK = 128
CONV_WIDTH = 31
CONV_PAD = CONV_WIDTH // 2
CHUNK = 128
N_GROUPS = 16
GROUP_DIM = D_MODEL // N_GROUPS
D_FF = 4 * D_MODEL
N_MOD = 6
EPS = 1e-6

kernel_name = "hybrid_mla_conformer_gmlp_dit"


def rms_norm(x, g):
    xf = x.astype(jnp.float32)
    y = xf * lax.rsqrt(jnp.mean(xf * xf, axis=-1, keepdims=True) + EPS)
    return (y * g.astype(jnp.float32)).astype(x.dtype)


def layer_norm(x, g, b):
    xf = x.astype(jnp.float32)
    mu = jnp.mean(xf, axis=-1, keepdims=True)
    var = jnp.mean(jnp.square(xf - mu), axis=-1, keepdims=True)
    y = (xf - mu) * lax.rsqrt(var + EPS) * g.astype(jnp.float32) + b.astype(jnp.float32)
    return y.astype(x.dtype)


def ada_mod(cond, w, b):
    m = jax.nn.silu(cond) @ w + b
    return jnp.split(m, N_MOD, axis=-1)


def modulate(h, shift, scale):
    return h * (1.0 + scale) + shift


def axial_rope_tables(n):
    t = jnp.arange(n, dtype=jnp.int32)
    row = (t // GRID_W).astype(jnp.float32)
    col = (t % GRID_W).astype(jnp.float32)
    half = QK_ROPE_DIM // 2
    inv = ROPE_THETA ** (-jnp.arange(0, half, 2, dtype=jnp.float32) / half)
    ang = jnp.stack([row[:, None] * inv, col[:, None] * inv], axis=1)
    return jnp.cos(ang), jnp.sin(ang)


def apply_rope(x, cos, sin):
    xs = x.reshape(x.shape[:-1] + (2, 2, QK_ROPE_DIM // 4))
    x1, x2 = xs[..., 0, :], xs[..., 1, :]
    co, si = cos.astype(x.dtype), sin.astype(x.dtype)
    out = jnp.stack([x1 * co - x2 * si, x2 * co + x1 * si], axis=-2)
    return out.reshape(x.shape)


def mla_queries(h, w_dq, q_norm, w_uq):
    b, n, _ = h.shape
    q = rms_norm(h @ w_dq, q_norm) @ w_uq
    return q.reshape(b, n, N_HEADS, QK_DIM)


def mla_keys_values(h, w_dkv, kv_norm, w_ukv):
    b, n, _ = h.shape
    kv_a = h @ w_dkv
    c_kv, k_pe = kv_a[..., :KV_LORA_RANK], kv_a[..., KV_LORA_RANK:]
    kv = (rms_norm(c_kv, kv_norm) @ w_ukv).reshape(b, n, N_HEADS, QK_NOPE_DIM + V_HEAD_DIM)
    return kv[..., :QK_NOPE_DIM], k_pe, kv[..., QK_NOPE_DIM:]


def assemble_k(k_nope, k_pe):
    k_pe_h = jnp.broadcast_to(k_pe[:, :, None, :], k_nope.shape[:-1] + (QK_ROPE_DIM,))
    return jnp.concatenate([k_nope, k_pe_h], axis=-1)


def attend(q, k, v):
    s = jnp.einsum('bqhd,bkhd->bhqk', q * ATTN_SCALE, k).astype(jnp.float32)
    p = jax.nn.softmax(s, axis=-1).astype(v.dtype)
    return jnp.einsum('bhqk,bkhd->bqhd', p, v)


def mla_mixer(h_lat, h_ctx, w_dq, q_norm, w_uq, w_dkv, kv_norm, w_ukv, w_o, ctx_out):
    b, n, _ = h_lat.shape
    cos, sin = axial_rope_tables(n)
    q_l = mla_queries(h_lat, w_dq, q_norm, w_uq)
    q_l = jnp.concatenate([q_l[..., :QK_NOPE_DIM],
                           apply_rope(q_l[..., QK_NOPE_DIM:], cos[:, None], sin[:, None])], axis=-1)
    kn_l, kpe_l, v_l = mla_keys_values(h_lat, w_dkv, kv_norm, w_ukv)
    k_l = assemble_k(kn_l, apply_rope(kpe_l, cos, sin))
    kn_c, kpe_c, v_c = mla_keys_values(h_ctx, w_dkv, kv_norm, w_ukv)
    k_c = assemble_k(kn_c, kpe_c)
    k_all = jnp.concatenate([k_c, k_l], axis=1)
    v_all = jnp.concatenate([v_c, v_l], axis=1)
    nb = n // Q_BLOCK
    qb = q_l.reshape(b, nb, Q_BLOCK, N_HEADS, QK_DIM).swapaxes(0, 1)
    ob = lax.map(lambda qblk: attend(qblk, k_all, v_all), qb)
    o_lat = ob.swapaxes(0, 1).reshape(b, n, N_HEADS * V_HEAD_DIM) @ w_o
    if not ctx_out:
        return o_lat, None
    q_c = mla_queries(h_ctx, w_dq, q_norm, w_uq)
    lc = h_ctx.shape[1]
    o_ctx = attend(q_c, k_c, v_c).reshape(b, lc, N_HEADS * V_HEAD_DIM) @ w_o
    return o_lat, o_ctx


def conformer_conv(h, w_pw1, b_pw1, w_dw, b_dw, ln_g, ln_b, w_pw2, b_pw2):
    a = h @ w_pw1 + b_pw1
    g = a[..., :D_MODEL] * jax.nn.sigmoid(a[..., D_MODEL:])
    y = lax.conv_general_dilated(
        g, w_dw[:, None, :], window_strides=(1,), padding=[(CONV_PAD, CONV_PAD)],
        dimension_numbers=('NWC', 'WIO', 'NWC'), feature_group_count=D_MODEL) + b_dw
    y = jax.nn.silu(layer_norm(y, ln_g, ln_b))
    return y @ w_pw2 + b_pw2


def chunk_gmlp(h, w_uv, b_uv, ln_g, ln_b, w_s, b_s, w_out, b_out):
    b, n, _ = h.shape
    a = jax.nn.gelu(h @ w_uv + b_uv, approximate=False)
    u, v = a[..., :D_MODEL], a[..., D_MODEL:]
    v = layer_norm(v, ln_g, ln_b).reshape(b, n // CHUNK, CHUNK, N_GROUPS, GROUP_DIM)
    sv = jnp.einsum('gpq,bcqgd->bcpgd', w_s, v) + b_s.T[:, :, None]
    return (u * sv.reshape(b, n, D_MODEL)) @ w_out + b_out


def sq_relu_mlp(h, w1, w2):
    return jnp.square(jax.nn.relu(h @ w1)) @ w2


def setup_inputs(seed: int = 0) -> dict:
    key = jax.random.key(seed)
    ks = iter(jax.random.split(key, 48))
    D = D_MODEL

    def nrm(shape, scale):
        return jax.random.normal(next(ks), shape, jnp.float32) * scale

    def gain(shape):
        return 1.0 + nrm(shape, 0.02)

    return {
        "x": nrm((BATCH, SEQ, D), 1.0),
        "c": nrm((BATCH, D), 1.0),
        "ctx": nrm((BATCH, CTX_LEN, D), 1.0),
        "c_ctx": nrm((D,), 1.0),
        "ada_w": nrm((DEPTH, D, N_MOD * D), 0.5 * D ** -0.5),
        "ada_b": nrm((DEPTH, N_MOD * D), 0.02),
        "norm_mix_pre": gain((DEPTH, D)),
        "norm_mix_post": gain((DEPTH, D)),
        "norm_ffn_pre": gain((DEPTH, D)),
        "norm_ffn_post": gain((DEPTH, D)),
        "mla_w_dq": nrm((N_MLA, D, Q_LORA_RANK), D ** -0.5),
        "mla_q_norm": gain((N_MLA, Q_LORA_RANK)),
        "mla_w_uq": nrm((N_MLA, Q_LORA_RANK, N_HEADS * QK_DIM), Q_LORA_RANK ** -0.5),
        "mla_w_dkv": nrm((N_MLA, D, KV_LORA_RANK + QK_ROPE_DIM), D ** -0.5),
        "mla_kv_norm": gain((N_MLA, KV_LORA_RANK)),
        "mla_w_ukv": nrm((N_MLA, KV_LORA_RANK, N_HEADS * (QK_NOPE_DIM + V_HEAD_DIM)), KV_LORA_RANK ** -0.5),
        "mla_w_o": nrm((N_MLA, N_HEADS * V_HEAD_DIM, D), (N_HEADS * V_HEAD_DIM) ** -0.5),
        "conv_w_pw1": nrm((N_CONV, D, 2 * D), D ** -0.5),
        "conv_b_pw1": nrm((N_CONV, 2 * D), 0.02),
        "conv_w_dw": nrm((N_CONV, CONV_WIDTH, D), CONV_WIDTH ** -0.5),
        "conv_b_dw": nrm((N_CONV, D), 0.02),
        "conv_ln_g": gain((N_CONV, D)),
        "conv_ln_b": nrm((N_CONV, D), 0.02),
        "conv_w_pw2": nrm((N_CONV, D, D), D ** -0.5),
        "conv_b_pw2": nrm((N_CONV, D), 0.02),
        "gmlp_w_uv": nrm((N_GMLP, D, 2 * D), D ** -0.5),
        "gmlp_b_uv": nrm((N_GMLP, 2 * D), 0.02),
        "gmlp_ln_g": gain((N_GMLP, D)),
        "gmlp_ln_b": nrm((N_GMLP, D), 0.02),
        "gmlp_w_s": nrm((N_GMLP, N_GROUPS, CHUNK, CHUNK), CHUNK ** -0.5),
        "gmlp_b_s": gain((N_GMLP, N_GROUPS, CHUNK)),
        "gmlp_w_out": nrm((N_GMLP, D, D), D ** -0.5),
        "gmlp_b_out": nrm((N_GMLP, D), 0.02),
        "ffn_w1": nrm((DEPTH, D, D_FF), D ** -0.5),
        "ffn_w2": nrm((DEPTH, D_FF, D), D_FF ** -0.5),
    }


def reference(x, c, ctx, c_ctx, ada_w, ada_b, norm_mix_pre, norm_mix_post, norm_ffn_pre,
              norm_ffn_post, mla_w_dq, mla_q_norm, mla_w_uq, mla_w_dkv, mla_kv_norm, mla_w_ukv,
              mla_w_o, conv_w_pw1, conv_b_pw1, conv_w_dw, conv_b_dw, conv_ln_g, conv_ln_b,
              conv_w_pw2, conv_b_pw2, gmlp_w_uv, gmlp_b_uv, gmlp_ln_g, gmlp_ln_b, gmlp_w_s,
              gmlp_b_s, gmlp_w_out, gmlp_b_out, ffn_w1, ffn_w2):
    x_lat, x_ctx = x, ctx
    for i in range(DEPTH):
        last = i == DEPTH - 1
        kind, j = i % N_MIXERS, i // N_MIXERS
        need_ctx = (not last) or kind == 0
        m_lat = ada_mod(c[:, None, :], ada_w[i], ada_b[i])
        h_lat = modulate(rms_norm(x_lat, norm_mix_pre[i]), m_lat[0], m_lat[1])
        if need_ctx:
            m_ctx = ada_mod(c_ctx, ada_w[i], ada_b[i])
            h_ctx = modulate(rms_norm(x_ctx, norm_mix_pre[i]), m_ctx[0], m_ctx[1])
        o_ctx = None
        if kind == 0:
            o_lat, o_ctx = mla_mixer(h_lat, h_ctx, mla_w_dq[j], mla_q_norm[j], mla_w_uq[j],
                                     mla_w_dkv[j], mla_kv_norm[j], mla_w_ukv[j], mla_w_o[j],
                                     not last)
        elif kind == 1:
            conv_p = (conv_w_pw1[j], conv_b_pw1[j], conv_w_dw[j], conv_b_dw[j], conv_ln_g[j],
                      conv_ln_b[j], conv_w_pw2[j], conv_b_pw2[j])
            o_lat = conformer_conv(h_lat, *conv_p)
            if not last:
                o_ctx = conformer_conv(h_ctx, *conv_p)
        else:
            gm_p = (gmlp_w_uv[j], gmlp_b_uv[j], gmlp_ln_g[j], gmlp_ln_b[j], gmlp_w_s[j],
                    gmlp_b_s[j], gmlp_w_out[j], gmlp_b_out[j])
            o_lat = chunk_gmlp(h_lat, *gm_p)
            if not last:
                o_ctx = chunk_gmlp(h_ctx, *gm_p)
        x_lat = x_lat + m_lat[2] * rms_norm(o_lat, norm_mix_post[i])
        f_lat = modulate(rms_norm(x_lat, norm_ffn_pre[i]), m_lat[3], m_lat[4])
        x_lat = x_lat + m_lat[5] * rms_norm(sq_relu_mlp(f_lat, ffn_w1[i], ffn_w2[i]), norm_ffn_post[i])
        if not last:
            x_ctx = x_ctx + m_ctx[2] * rms_norm(o_ctx, norm_mix_post[i])
            f_ctx = modulate(rms_norm(x_ctx, norm_ffn_pre[i]), m_ctx[3], m_ctx[4])
            x_ctx = x_ctx + m_ctx[5] * rms_norm(sq_relu_mlp(f_ctx, ffn_w1[i], ffn_w2[i]), norm_ffn_post[i])
    return x_lat
```

```python
import functools

import jax
import jax.numpy as jnp
import numpy as np
from jax import lax
from jax.experimental import pallas as pl
from jax.experimental.pallas import tpu as pltpu

D_MODEL = 2048
BATCH = 2
SEQ = 4096
DEPTH = 4
CTX_LEN = 256
GRID_W = 64
N_MIXERS = 3
N_HEADS = 16
QK_NOPE_DIM = 128
QK_ROPE_DIM = 64
QK_DIM = QK_NOPE_DIM + QK_ROPE_DIM
V_HEAD_DIM = 128
Q_LORA_RANK = 768
KV_LORA_RANK = 512
ROPE_THETA = 10000.0
ATTN_SCALE = QK_DIM ** -0.5
CONV_WIDTH = 31
CONV_PAD = CONV_WIDTH // 2
CHUNK = 128
N_GROUPS = 16
GROUP_DIM = D_MODEL // N_GROUPS
D_FF = 4 * D_MODEL
N_MOD = 6
EPS = 1e-6

N_LAT = BATCH * SEQ
N_CTX = BATCH * CTX_LEN
N_TOK = N_LAT + N_CTX
N_SETS = 3
HEAD_PAD = 256
LANES = 128
HALO = 16
VMEM_LIMIT = 56 * 1024 * 1024

F32 = jnp.float32
BF16 = jnp.bfloat16


def _dot(a, b):
    return jnp.dot(a, b, preferred_element_type=F32)


def _rms(x, g):
    return x * lax.rsqrt(jnp.mean(x * x, axis=-1, keepdims=True) + EPS) * g


def _layer_norm(x, g, b):
    mu = jnp.mean(x, axis=-1, keepdims=True)
    xc = x - mu
    var = jnp.mean(xc * xc, axis=-1, keepdims=True)
    return xc * lax.rsqrt(var + EPS) * g + b


def _sigmoid(x):
    return 1.0 / (1.0 + jnp.exp(-x))


def _gelu(x):
    return 0.5 * x * (1.0 + lax.erf(x * np.float32(np.sqrt(0.5))))


def _premix(x, mod, g, shift_idx):
    return _rms(x, g) * (1.0 + mod[shift_idx + 1:shift_idx + 2]) + mod[shift_idx:shift_idx + 1]


def _params(n_grid_axes):
    return pltpu.CompilerParams(
        dimension_semantics=("arbitrary",) * n_grid_axes,
        vmem_limit_bytes=VMEM_LIMIT)


def _resident(shape):
    nd = len(shape)
    return pl.BlockSpec(shape, lambda *_: (0,) * nd, pipeline_mode=pl.Buffered(1))


def _mod_spec(tm):
    return pl.BlockSpec((1, N_MOD, D_MODEL),
                        lambda i, *_: (jnp.minimum(i * tm // SEQ, N_SETS - 1), 0, 0))


def _row_spec(tm, width):
    return pl.BlockSpec((tm, width), lambda i, *_: (i, 0))


def _ada_kernel(cond_ref, w_ref, b_ref, o_ref):
    c = cond_ref[...]
    s = (c * _sigmoid(c)).astype(BF16)
    o_ref[0] = _dot(s, w_ref[0].astype(BF16)) + b_ref[0]


def _ada_all(cond, ada_w, ada_b):
    tn = 1024
    n_out = N_MOD * D_MODEL
    return pl.pallas_call(
        _ada_kernel,
        grid=(DEPTH, n_out // tn),
        in_specs=[
            pl.BlockSpec((8, D_MODEL), lambda l, j: (0, 0)),
            pl.BlockSpec((1, D_MODEL, tn), lambda l, j: (l, 0, j)),
            pl.BlockSpec((1, 1, tn), lambda l, j: (l, 0, j)),
        ],
        out_specs=pl.BlockSpec((1, 8, tn), lambda l, j: (l, 0, j)),
        out_shape=jax.ShapeDtypeStruct((DEPTH, 8, n_out), F32),
        compiler_params=_params(2),
        name="ada_mod",
    )(cond, ada_w, ada_b.reshape(DEPTH, 1, n_out))


def _post_residual(y, x, mod, g_post, gate_idx):
    return x + mod[gate_idx:gate_idx + 1] * _rms(y, g_post)


def _ffn_kernel(n_f, x_ref, mod_ref, gpre_ref, gpost_ref, w1_ref, w2_ref, o_ref, h_scr, acc_scr):
    f = pl.program_id(1)

    @pl.when(f == 0)
    def _():
        h_scr[...] = _premix(x_ref[...], mod_ref[0], gpre_ref[...], 3).astype(BF16)

    a = _dot(h_scr[...], w1_ref[...])
    a = jnp.square(jnp.maximum(a, 0.0)).astype(BF16)
    part = _dot(a, w2_ref[...])

    @pl.when(f == 0)
    def _():
        acc_scr[...] = part

    @pl.when(f > 0)
    def _():
        acc_scr[...] += part

    @pl.when(f == n_f - 1)
    def _():
        o_ref[...] = _post_residual(acc_scr[...], x_ref[...], mod_ref[0], gpost_ref[...], 5)


def _ffn(x, mods, g_pre, g_post, w1, w2, n_rows):
    tm, tf = 512, 512
    n_f = D_FF // tf
    return pl.pallas_call(
        functools.partial(_ffn_kernel, n_f),
        grid=(n_rows // tm, n_f),
        in_specs=[
            _row_spec(tm, D_MODEL),
            _mod_spec(tm),
            pl.BlockSpec((1, D_MODEL), lambda i, f: (0, 0)),
            pl.BlockSpec((1, D_MODEL), lambda i, f: (0, 0)),
            pl.BlockSpec((D_MODEL, tf), lambda i, f: (0, f)),
            pl.BlockSpec((tf, D_MODEL), lambda i, f: (f, 0)),
        ],
        out_specs=_row_spec(tm, D_MODEL),
        out_shape=jax.ShapeDtypeStruct((n_rows, D_MODEL), F32),
        scratch_shapes=[pltpu.VMEM((tm, D_MODEL), BF16), pltpu.VMEM((tm, D_MODEL), F32)],
        compiler_params=_params(2),
        name="ffn",
    )(x, mods, g_pre, g_post, w1, w2)


def _mix_out_kernel(a_ref, w_ref, b_ref, x_ref, mod_ref, gpost_ref, o_ref):
    y = _dot(a_ref[...], w_ref[...]) + b_ref[...]
    o_ref[...] = _post_residual(y, x_ref[...], mod_ref[0], gpost_ref[...], 2)


def _mix_out(a, w, b, x, mods, g_post, n_rows):
    tm = 256
    return pl.pallas_call(
        _mix_out_kernel,
        grid=(n_rows // tm,),
        in_specs=[
            _row_spec(tm, D_MODEL),
            _resident((D_MODEL, D_MODEL)),
            _resident((1, D_MODEL)),
            _row_spec(tm, D_MODEL),
            _mod_spec(tm),
            _resident((1, D_MODEL)),
        ],
        out_specs=_row_spec(tm, D_MODEL),
        out_shape=jax.ShapeDtypeStruct((n_rows, D_MODEL), F32),
        compiler_params=_params(1),
        name="mix_out",
    )(a, w, b, x, mods, g_post)


def _rope_tables():
    t = jnp.arange(SEQ, dtype=jnp.int32)
    row = (t // GRID_W).astype(F32)
    col = (t % GRID_W).astype(F32)
    half = QK_ROPE_DIM // 2
    inv = ROPE_THETA ** (-jnp.arange(0, half, 2, dtype=F32) / half)
    ang_r, ang_c = row[:, None] * inv, col[:, None] * inv
    cr, sr, cc, sc = jnp.cos(ang_r), jnp.sin(ang_r), jnp.cos(ang_c), jnp.sin(ang_c)
    zeros = jnp.zeros((SEQ, QK_ROPE_DIM), F32)
    cos_tab = jnp.concatenate([cr, cr, cc, cc, zeros], axis=1)
    sin_tab = jnp.concatenate([-sr, sr, -sc, sc, zeros], axis=1)
    ident = jnp.concatenate([jnp.ones((CTX_LEN, QK_ROPE_DIM), F32),
                             jnp.zeros((CTX_LEN, QK_ROPE_DIM), F32)], axis=1)
    cos_tab = jnp.concatenate([cos_tab, ident], axis=0)
    sin_tab = jnp.concatenate([sin_tab, jnp.zeros((CTX_LEN, LANES), F32)], axis=0)
    return cos_tab, sin_tab


def _swap_half_lanes(w):
    perm = np.arange(QK_ROPE_DIM) ^ (QK_ROPE_DIM // 4)
    return w[..., perm]


def _mla_weights(w_dq, w_uq, w_dkv, w_ukv, w_o):
    uq = w_uq.reshape(Q_LORA_RANK, N_HEADS, QK_DIM)
    rope = uq[..., QK_NOPE_DIM:]
    uq = jnp.concatenate([uq[..., :QK_NOPE_DIM], rope, _swap_half_lanes(rope)], axis=-1)
    kpe = w_dkv[:, KV_LORA_RANK:]
    dkv = jnp.concatenate([w_dkv[:, :KV_LORA_RANK], kpe, _swap_half_lanes(kpe)], axis=-1)
    return (w_dq.astype(BF16), uq.reshape(Q_LORA_RANK, N_HEADS * HEAD_PAD).astype(BF16),
            dkv.astype(BF16), w_ukv.astype(BF16), w_o.astype(BF16))


def _rope(blk, cos_tab, sin_tab):
    return blk * cos_tab + pltpu.roll(blk, LANES // 2, axis=1) * sin_tab


def _mla_proj_kernel(x_ref, mod_ref, gpre_ref, wdq_ref, qn_ref, wuq_ref, wdkv_ref, kvn_ref,
                     wukv_ref, cos_ref, sin_ref, q_ref, k_ref, v_ref):
    h = _premix(x_ref[...], mod_ref[0], gpre_ref[...], 0).astype(BF16)
    cos_tab, sin_tab = cos_ref[...], sin_ref[...]
    q_lat = _rms(_dot(h, wdq_ref[...]), qn_ref[...]).astype(BF16)
    kv_a = _dot(h, wdkv_ref[...])
    c_kv = _rms(kv_a[:, :KV_LORA_RANK], kvn_ref[...]).astype(BF16)
    k_pe = _rope(kv_a[:, KV_LORA_RANK:], cos_tab, sin_tab).astype(BF16)
    for hd in range(N_HEADS):
        lo = hd * HEAD_PAD
        q = _dot(q_lat, wuq_ref[:, lo:lo + HEAD_PAD])
        q_ref[:, lo:lo + LANES] = (q[:, :LANES] * ATTN_SCALE).astype(BF16)
        q_ref[:, lo + LANES:lo + HEAD_PAD] = (
            _rope(q[:, LANES:], cos_tab, sin_tab) * ATTN_SCALE).astype(BF16)
        kv = _dot(c_kv, wukv_ref[:, lo:lo + HEAD_PAD])
        k_ref[:, lo:lo + LANES] = kv[:, :LANES].astype(BF16)
        k_ref[:, lo + LANES:lo + HEAD_PAD] = k_pe
        v_ref[:, hd * LANES:(hd + 1) * LANES] = kv[:, LANES:].astype(BF16)


def _mla_proj(x, mods, g_pre, w_dq, q_norm, w_uq, w_dkv, kv_norm, w_ukv, cos_tab, sin_tab):
    tm = 256
    lat_tiles, seq_tiles = N_LAT // tm, SEQ // tm
    tab_spec = pl.BlockSpec(
        (tm, LANES), lambda i: (jnp.where(i < lat_tiles, i % seq_tiles, seq_tiles), 0))
    wide = N_HEADS * HEAD_PAD
    return pl.pallas_call(
        _mla_proj_kernel,
        grid=(N_TOK // tm,),
        in_specs=[
            _row_spec(tm, D_MODEL),
            _mod_spec(tm),
            _resident((1, D_MODEL)),
            _resident((D_MODEL, Q_LORA_RANK)),
            _resident((1, Q_LORA_RANK)),
            _resident((Q_LORA_RANK, wide)),
            _resident((D_MODEL, KV_LORA_RANK + LANES)),
            _resident((1, KV_LORA_RANK)),
            _resident((KV_LORA_RANK, wide)),
            tab_spec,
            tab_spec,
        ],
        out_specs=[_row_spec(tm, wide), _row_spec(tm, wide), _row_spec(tm, D_MODEL)],
        out_shape=[jax.ShapeDtypeStruct((N_TOK, wide), BF16),
                   jax.ShapeDtypeStruct((N_TOK, wide), BF16),
                   jax.ShapeDtypeStruct((N_TOK, D_MODEL), BF16)],
        compiler_params=_params(1),
        name="mla_proj",
    )(x, mods, g_pre, w_dq, q_norm, w_uq, w_dkv, kv_norm, w_ukv, cos_tab, sin_tab)


def _attn_kernel(lat_q_tiles, tk, q_ref, kc_ref, kl_ref, vc_ref, vl_ref, o_ref,
                 m_scr, l_scr, acc_scr):
    qt = pl.program_id(2)
    q = q_ref[...]

    def scores(k):
        return lax.dot_general(q, k, (((1,), (1,)), ((), ())), preferred_element_type=F32)

    s = scores(kc_ref[...])
    m0 = jnp.max(s, axis=1, keepdims=True)
    p = jnp.exp(s - m0)
    m_scr[...] = jnp.broadcast_to(m0, m_scr.shape)
    l_scr[...] = jnp.broadcast_to(jnp.sum(p, axis=1, keepdims=True), l_scr.shape)
    acc_scr[...] = _dot(p.astype(BF16), vc_ref[...])

    def lat_chunks():
        for c in range(SEQ // tk):
            s = scores(kl_ref[c * tk:(c + 1) * tk, :])
            m_prev = m_scr[...]
            m_next = jnp.maximum(m_prev, jnp.max(s, axis=1, keepdims=True))
            alpha = jnp.exp(m_prev - m_next)
            p = jnp.exp(s - m_next[:, :1])
            l_scr[...] = alpha * l_scr[...] + jnp.sum(p, axis=1, keepdims=True)
            acc_scr[...] = alpha * acc_scr[...] + _dot(p.astype(BF16),
                                                       vl_ref[c * tk:(c + 1) * tk, :])
            m_scr[...] = m_next

    pl.when(qt < lat_q_tiles)(lat_chunks)
    o_ref[...] = (acc_scr[...] / l_scr[...]).astype(BF16)


def _attention(q, k, v, with_ctx_queries):
    tq, tk = 256, 512
    lat_q_tiles = SEQ // tq
    n_qt = lat_q_tiles + (1 if with_ctx_queries else 0)
    ctx_blk = N_LAT // CTX_LEN

    def q_map(b, h, t):
        return (jnp.where(t < lat_q_tiles, b * lat_q_tiles + t, N_LAT // tq + b), h)

    return pl.pallas_call(
        functools.partial(_attn_kernel, lat_q_tiles, tk),
        grid=(BATCH, N_HEADS, n_qt),
        in_specs=[
            pl.BlockSpec((tq, HEAD_PAD), q_map),
            pl.BlockSpec((CTX_LEN, HEAD_PAD), lambda b, h, t: (ctx_blk + b, h)),
            pl.BlockSpec((SEQ, HEAD_PAD), lambda b, h, t: (b, h)),
            pl.BlockSpec((CTX_LEN, V_HEAD_DIM), lambda b, h, t: (ctx_blk + b, h)),
            pl.BlockSpec((SEQ, V_HEAD_DIM), lambda b, h, t: (b, h)),
        ],
        out_specs=pl.BlockSpec((tq, V_HEAD_DIM), q_map),
        out_shape=jax.ShapeDtypeStruct((N_TOK if with_ctx_queries else N_LAT,
                                        N_HEADS * V_HEAD_DIM), BF16),
        scratch_shapes=[pltpu.VMEM((tq, LANES), F32), pltpu.VMEM((tq, LANES), F32),
                        pltpu.VMEM((tq, V_HEAD_DIM), F32)],
        compiler_params=_params(3),
        name="mla_attention",
    )(q, k, k, v, v)


def _conv_in_kernel(x_ref, mod_ref, gpre_ref, wa_ref, wg_ref, ba_ref, bg_ref, o_ref):
    h = _premix(x_ref[...], mod_ref[0], gpre_ref[...], 0).astype(BF16)
    a = _dot(h, wa_ref[...]) + ba_ref[...]
    gate = _dot(h, wg_ref[...]) + bg_ref[...]
    o_ref[...] = a * _sigmoid(gate)


def _conv_in(x, mods, g_pre, w_a, w_g, b_a, b_g):
    tm = 256
    return pl.pallas_call(
        _conv_in_kernel,
        grid=(N_TOK // tm,),
        in_specs=[
            _row_spec(tm, D_MODEL),
            _mod_spec(tm),
            _resident((1, D_MODEL)),
            _resident((D_MODEL, D_MODEL)),
            _resident((D_MODEL, D_MODEL)),
            _resident((1, D_MODEL)),
            _resident((1, D_MODEL)),
        ],
        out_specs=_row_spec(tm, D_MODEL),
        out_shape=jax.ShapeDtypeStruct((N_TOK, D_MODEL), F32),
        compiler_params=_params(1),
        name="conv_in_glu",
    )(x, mods, g_pre, w_a, w_g, b_a, b_g)


def _conv_out_kernel(tm, g_ref, left_ref, right_ref, wdw_ref, bdw_ref, lng_ref, lnb_ref, w2_ref,
                     b2_ref, x_ref, mod_ref, gpost_ref, o_ref, pad_scr, y_scr):
    i = pl.program_id(0)
    seg_tiles = SEQ // tm
    in_lat = i < N_LAT // tm
    has_left = jnp.logical_and(in_lat, i % seg_tiles != 0)
    has_right = jnp.logical_and(in_lat, i % seg_tiles != seg_tiles - 1)
    pad_scr[0:HALO, :] = jnp.where(has_left, left_ref[...], 0.0)
    pad_scr[HALO:HALO + tm, :] = g_ref[...]
    pad_scr[HALO + tm:HALO + tm + HALO, :] = jnp.where(has_right, right_ref[...], 0.0)

    rows, cols = 128, 256
    first = HALO - CONV_PAD
    for r0 in range(0, tm, rows):
        for c0 in range(0, D_MODEL, cols):
            acc = jnp.zeros((rows, cols), F32)
            for j in range(CONV_WIDTH):
                acc += (wdw_ref[j:j + 1, c0:c0 + cols]
                        * pad_scr[r0 + first + j:r0 + first + j + rows, c0:c0 + cols])
            y_scr[r0:r0 + rows, c0:c0 + cols] = acc + bdw_ref[:, c0:c0 + cols]

    y = _layer_norm(y_scr[...], lng_ref[...], lnb_ref[...])
    y = (y * _sigmoid(y)).astype(BF16)
    out = _dot(y, w2_ref[...]) + b2_ref[...]
    o_ref[...] = _post_residual(out, x_ref[...], mod_ref[0], gpost_ref[...], 2)


def _conv_out(g, w_dw, b_dw, ln_g, ln_b, w2, b2, x, mods, g_post, n_rows):
    tm = 256
    halo_per_tile = tm // HALO
    last_halo = N_TOK // HALO - 1
    return pl.pallas_call(
        functools.partial(_conv_out_kernel, tm),
        grid=(n_rows // tm,),
        in_specs=[
            _row_spec(tm, D_MODEL),
            pl.BlockSpec((HALO, D_MODEL), lambda i: (jnp.maximum(i * halo_per_tile - 1, 0), 0)),
            pl.BlockSpec((HALO, D_MODEL),
                         lambda i: (jnp.minimum((i + 1) * halo_per_tile, last_halo), 0)),
            _resident((CONV_WIDTH, D_MODEL)),
            _resident((1, D_MODEL)),
            _resident((1, D_MODEL)),
            _resident((1, D_MODEL)),
            _resident((D_MODEL, D_MODEL)),
            _resident((1, D_MODEL)),
            _row_spec(tm, D_MODEL),
            _mod_spec(tm),
            _resident((1, D_MODEL)),
        ],
        out_specs=_row_spec(tm, D_MODEL),
        out_shape=jax.ShapeDtypeStruct((n_rows, D_MODEL), F32),
        scratch_shapes=[pltpu.VMEM((tm + 2 * HALO, D_MODEL), F32), pltpu.VMEM((tm, D_MODEL), F32)],
        compiler_params=_params(1),
        name="conv_dw_out",
    )(g, g, g, w_dw, b_dw, ln_g, ln_b, w2, b2, x, mods, g_post)


def _gmlp_in_kernel(normalize, x_ref, mod_ref, gpre_ref, w_ref, b_ref, lng_ref, lnb_ref, o_ref):
    h = _premix(x_ref[...], mod_ref[0], gpre_ref[...], 0).astype(BF16)
    a = _gelu(_dot(h, w_ref[...]) + b_ref[...])
    if normalize:
        a = _layer_norm(a, lng_ref[...], lnb_ref[...])
    o_ref[...] = a.astype(o_ref.dtype)


def _gmlp_in(x, mods, g_pre, w, b, ln_g, ln_b, normalize, out_dtype):
    tm = 256
    return pl.pallas_call(
        functools.partial(_gmlp_in_kernel, normalize),
        grid=(N_TOK // tm,),
        in_specs=[
            _row_spec(tm, D_MODEL),
            _mod_spec(tm),
            _resident((1, D_MODEL)),
            _resident((D_MODEL, D_MODEL)),
            _resident((1, D_MODEL)),
            _resident((1, D_MODEL)),
            _resident((1, D_MODEL)),
        ],
        out_specs=_row_spec(tm, D_MODEL),
        out_shape=jax.ShapeDtypeStruct((N_TOK, D_MODEL), out_dtype),
        compiler_params=_params(1),
        name="gmlp_in",
    )(x, mods, g_pre, w, b, ln_g, ln_b)


def _gmlp_out_kernel(tm, u_ref, v_ref, ws_ref, bs_ref, w_ref, b_ref, x_ref, mod_ref, gpost_ref,
                     o_ref, t_scr):
    for g in range(N_GROUPS):
        c0 = g * GROUP_DIM
        w_s = ws_ref[g]
        for r0 in range(0, tm, CHUNK):
            sv = _dot(w_s, v_ref[r0:r0 + CHUNK, c0:c0 + GROUP_DIM]) + bs_ref[:, c0:c0 + GROUP_DIM]
            t_scr[r0:r0 + CHUNK, c0:c0 + GROUP_DIM] = (
                u_ref[r0:r0 + CHUNK, c0:c0 + GROUP_DIM] * sv).astype(BF16)
    out = _dot(t_scr[...], w_ref[...]) + b_ref[...]
    o_ref[...] = _post_residual(out, x_ref[...], mod_ref[0], gpost_ref[...], 2)


def _gmlp_out(u, v, w_s, b_s_wide, w, b, x, mods, g_post, n_rows):
    tm = 256
    return pl.pallas_call(
        functools.partial(_gmlp_out_kernel, tm),
        grid=(n_rows // tm,),
        in_specs=[
            _row_spec(tm, D_MODEL),
            _row_spec(tm, D_MODEL),
            _resident((N_GROUPS, CHUNK, CHUNK)),
            _resident((CHUNK, D_MODEL)),
            _resident((D_MODEL, D_MODEL)),
            _resident((1, D_MODEL)),
            _row_spec(tm, D_MODEL),
            _mod_spec(tm),
            _resident((1, D_MODEL)),
        ],
        out_specs=_row_spec(tm, D_MODEL),
        out_shape=jax.ShapeDtypeStruct((n_rows, D_MODEL), F32),
        scratch_shapes=[pltpu.VMEM((tm, D_MODEL), BF16)],
        compiler_params=_params(1),
        name="gmlp_spatial_out",
    )(u, v, w_s, b_s_wide, w, b, x, mods, g_post)


def _row(v):
    return v.reshape(1, -1)


def kernel(x, c, ctx, c_ctx, ada_w, ada_b, norm_mix_pre, norm_mix_post, norm_ffn_pre, norm_ffn_post, mla_w_dq, mla_q_norm, mla_w_uq, mla_w_dkv, mla_kv_norm, mla_w_ukv, mla_w_o, conv_w_pw1, conv_b_pw1, conv_w_dw, conv_b_dw, conv_ln_g, conv_ln_b, conv_w_pw2, conv_b_pw2, gmlp_w_uv, gmlp_b_uv, gmlp_ln_g, gmlp_ln_b, gmlp_w_s, gmlp_b_s, gmlp_w_out, gmlp_b_out, ffn_w1, ffn_w2):
    stream = jnp.concatenate([x.reshape(N_LAT, D_MODEL), ctx.reshape(N_CTX, D_MODEL)], axis=0)
    cond = jnp.concatenate([c, c_ctx[None, :], jnp.zeros((8 - BATCH - 1, D_MODEL), F32)], axis=0)
    mods_all = _ada_all(cond, ada_w, ada_b)[:, :N_SETS].reshape(DEPTH, N_SETS, N_MOD, D_MODEL)
    cos_tab, sin_tab = _rope_tables()
    zero_bias = jnp.zeros((1, D_MODEL), F32)

    for i in range(DEPTH):
        last = i == DEPTH - 1
        kind, j = i % N_MIXERS, i // N_MIXERS
        n_rows = N_LAT if last else N_TOK
        mods = mods_all[i]
        if kind == 0:
            w_dq, w_uq, w_dkv, w_ukv, w_o = _mla_weights(
                mla_w_dq[j], mla_w_uq[j], mla_w_dkv[j], mla_w_ukv[j], mla_w_o[j])
            q, k, v = _mla_proj(stream, mods, _row(norm_mix_pre[i]), w_dq, _row(mla_q_norm[j]),
                                w_uq, w_dkv, _row(mla_kv_norm[j]), w_ukv, cos_tab, sin_tab)
            o = _attention(q, k, v, with_ctx_queries=not last)
            stream = _mix_out(o, w_o, zero_bias, stream, mods, _row(norm_mix_post[i]), n_rows)
        elif kind == 1:
            w1 = conv_w_pw1[j].astype(BF16)
            g = _conv_in(stream, mods, _row(norm_mix_pre[i]), w1[:, :D_MODEL], w1[:, D_MODEL:],
                         _row(conv_b_pw1[j, :D_MODEL]), _row(conv_b_pw1[j, D_MODEL:]))
            stream = _conv_out(g, conv_w_dw[j], _row(conv_b_dw[j]), _row(conv_ln_g[j]),
                               _row(conv_ln_b[j]), conv_w_pw2[j].astype(BF16),
                               _row(conv_b_pw2[j]), stream, mods, _row(norm_mix_post[i]), n_rows)
        else:
            w_uv = gmlp_w_uv[j].astype(BF16)
            ln_g, ln_b = _row(gmlp_ln_g[j]), _row(gmlp_ln_b[j])
            u = _gmlp_in(stream, mods, _row(norm_mix_pre[i]), w_uv[:, :D_MODEL],
                         _row(gmlp_b_uv[j, :D_MODEL]), ln_g, ln_b, False, F32)
            v = _gmlp_in(stream, mods, _row(norm_mix_pre[i]), w_uv[:, D_MODEL:],
                         _row(gmlp_b_uv[j, D_MODEL:]), ln_g, ln_b, True, BF16)
            b_s_wide = jnp.repeat(gmlp_b_s[j].T, GROUP_DIM, axis=1)
            stream = _gmlp_out(u, v, gmlp_w_s[j].astype(BF16), b_s_wide,
                               gmlp_w_out[j].astype(BF16), _row(gmlp_b_out[j]), stream, mods,
                               _row(norm_mix_post[i]), n_rows)
        stream = _ffn(stream, mods, _row(norm_ffn_pre[i]), _row(norm_ffn_post[i]),
                      ffn_w1[i].astype(BF16), ffn_w2[i].astype(BF16), n_rows)
    return stream.reshape(BATCH, SEQ, D_MODEL)
```

```python
import functools

import jax
import jax.numpy as jnp
import numpy as np
from jax import lax
from jax.experimental import pallas as pl
from jax.experimental.pallas import tpu as pltpu

D_MODEL = 2048
BATCH = 2
SEQ = 4096
DEPTH = 4
CTX_LEN = 256
GRID_W = 64
N_MIXERS = 3
N_HEADS = 16
QK_NOPE_DIM = 128
QK_ROPE_DIM = 64
QK_DIM = QK_NOPE_DIM + QK_ROPE_DIM
V_HEAD_DIM = 128
Q_LORA_RANK = 768
KV_LORA_RANK = 512
ROPE_THETA = 10000.0
ATTN_SCALE = QK_DIM ** -0.5
CONV_WIDTH = 31
CONV_PAD = CONV_WIDTH // 2
CHUNK = 128
N_GROUPS = 16
GROUP_DIM = D_MODEL // N_GROUPS
D_FF = 4 * D_MODEL
N_MOD = 6
EPS = 1e-6

N_LAT = BATCH * SEQ
N_CTX = BATCH * CTX_LEN
N_TOK = N_LAT + N_CTX
N_SETS = 3
HEAD_PAD = 256
LANES = 128
SUBLANES = 8
HALO = 16
CONV_ROWS = 128
CONV_FIRST = HALO - CONV_PAD
CONV_SPAN = CONV_ROWS + SUBLANES * ((CONV_FIRST + CONV_WIDTH - 1) // SUBLANES)
VMEM_LIMIT = 56 * 1024 * 1024

F32 = jnp.float32
BF16 = jnp.bfloat16


def _dot(a, b):
    return jnp.dot(a, b, preferred_element_type=F32)


def _rms(x, g):
    return x * lax.rsqrt(jnp.mean(x * x, axis=-1, keepdims=True) + EPS) * g


def _layer_norm(x, g, b):
    mu = jnp.mean(x, axis=-1, keepdims=True)
    xc = x - mu
    var = jnp.mean(xc * xc, axis=-1, keepdims=True)
    return xc * lax.rsqrt(var + EPS) * g + b


def _sigmoid(x):
    return 1.0 / (1.0 + jnp.exp(-x))


def _gelu(x):
    return 0.5 * x * (1.0 + lax.erf(x * np.float32(np.sqrt(0.5))))


def _premix(x, mod, g, shift_idx):
    return _rms(x, g) * (1.0 + mod[shift_idx + 1:shift_idx + 2]) + mod[shift_idx:shift_idx + 1]


def _params(n_grid_axes):
    return pltpu.CompilerParams(
        dimension_semantics=("arbitrary",) * n_grid_axes,
        vmem_limit_bytes=VMEM_LIMIT)


def _resident(shape):
    nd = len(shape)
    return pl.BlockSpec(shape, lambda *_: (0,) * nd, pipeline_mode=pl.Buffered(1))


def _mod_spec(tm):
    return pl.BlockSpec((1, N_MOD, D_MODEL),
                        lambda i, *_: (jnp.minimum(i * tm // SEQ, N_SETS - 1), 0, 0))


def _row_spec(tm, width):
    return pl.BlockSpec((tm, width), lambda i, *_: (i, 0))


def _ada_kernel(cond_ref, w_ref, b_ref, o_ref):
    c = cond_ref[...]
    s = (c * _sigmoid(c)).astype(BF16)
    o_ref[0] = _dot(s, w_ref[0].astype(BF16)) + b_ref[0]


def _ada_all(cond, ada_w, ada_b):
    tn = 1024
    n_out = N_MOD * D_MODEL
    return pl.pallas_call(
        _ada_kernel,
        grid=(DEPTH, n_out // tn),
        in_specs=[
            pl.BlockSpec((8, D_MODEL), lambda l, j: (0, 0)),
            pl.BlockSpec((1, D_MODEL, tn), lambda l, j: (l, 0, j)),
            pl.BlockSpec((1, 1, tn), lambda l, j: (l, 0, j)),
        ],
        out_specs=pl.BlockSpec((1, 8, tn), lambda l, j: (l, 0, j)),
        out_shape=jax.ShapeDtypeStruct((DEPTH, 8, n_out), F32),
        compiler_params=_params(2),
        name="ada_mod",
    )(cond, ada_w, ada_b.reshape(DEPTH, 1, n_out))


def _post_residual(y, x, mod, g_post, gate_idx):
    return x + mod[gate_idx:gate_idx + 1] * _rms(y, g_post)


def _ffn_kernel(n_f, x_ref, mod_ref, gpre_ref, gpost_ref, w1_ref, w2_ref, o_ref, h_scr, acc_scr):
    f = pl.program_id(1)

    @pl.when(f == 0)
    def _():
        h_scr[...] = _premix(x_ref[...], mod_ref[0], gpre_ref[...], 3).astype(BF16)
        acc_scr[...] = jnp.zeros(acc_scr.shape, F32)

    a = _dot(h_scr[...], w1_ref[...])
    a = jnp.square(jnp.maximum(a, 0.0)).astype(BF16)
    acc_scr[...] += _dot(a, w2_ref[...])

    @pl.when(f == n_f - 1)
    def _():
        o_ref[...] = _post_residual(acc_scr[...], x_ref[...], mod_ref[0], gpost_ref[...], 5)


def _ffn(x, mods, g_pre, g_post, w1, w2, n_rows):
    tm, tf = 512, 1024
    n_f = D_FF // tf
    return pl.pallas_call(
        functools.partial(_ffn_kernel, n_f),
        grid=(n_rows // tm, n_f),
        in_specs=[
            _row_spec(tm, D_MODEL),
            _mod_spec(tm),
            pl.BlockSpec((1, D_MODEL), lambda i, f: (0, 0)),
            pl.BlockSpec((1, D_MODEL), lambda i, f: (0, 0)),
            pl.BlockSpec((D_MODEL, tf), lambda i, f: (0, f)),
            pl.BlockSpec((tf, D_MODEL), lambda i, f: (f, 0)),
        ],
        out_specs=_row_spec(tm, D_MODEL),
        out_shape=jax.ShapeDtypeStruct((n_rows, D_MODEL), F32),
        scratch_shapes=[pltpu.VMEM((tm, D_MODEL), BF16), pltpu.VMEM((tm, D_MODEL), F32)],
        compiler_params=_params(2),
        name="ffn",
    )(x, mods, g_pre, g_post, w1, w2)


def _mix_out_kernel(a_ref, w_ref, b_ref, x_ref, mod_ref, gpost_ref, o_ref):
    y = _dot(a_ref[...], w_ref[...]) + b_ref[...]
    o_ref[...] = _post_residual(y, x_ref[...], mod_ref[0], gpost_ref[...], 2)


def _mix_out(a, w, b, x, mods, g_post, n_rows):
    tm = 256
    return pl.pallas_call(
        _mix_out_kernel,
        grid=(n_rows // tm,),
        in_specs=[
            _row_spec(tm, D_MODEL),
            _resident((D_MODEL, D_MODEL)),
            _resident((1, D_MODEL)),
            _row_spec(tm, D_MODEL),
            _mod_spec(tm),
            _resident((1, D_MODEL)),
        ],
        out_specs=_row_spec(tm, D_MODEL),
        out_shape=jax.ShapeDtypeStruct((n_rows, D_MODEL), F32),
        compiler_params=_params(1),
        name="mix_out",
    )(a, w, b, x, mods, g_post)


def _rope_tables():
    t = jnp.arange(SEQ, dtype=jnp.int32)
    row = (t // GRID_W).astype(F32)
    col = (t % GRID_W).astype(F32)
    half = QK_ROPE_DIM // 2
    inv = ROPE_THETA ** (-jnp.arange(0, half, 2, dtype=F32) / half)
    ang_r, ang_c = row[:, None] * inv, col[:, None] * inv
    cr, sr, cc, sc = jnp.cos(ang_r), jnp.sin(ang_r), jnp.cos(ang_c), jnp.sin(ang_c)
    zeros = jnp.zeros((SEQ, QK_ROPE_DIM), F32)
    cos_tab = jnp.concatenate([cr, cr, cc, cc, zeros], axis=1)
    sin_tab = jnp.concatenate([-sr, sr, -sc, sc, zeros], axis=1)
    ident = jnp.concatenate([jnp.ones((CTX_LEN, QK_ROPE_DIM), F32),
                             jnp.zeros((CTX_LEN, QK_ROPE_DIM), F32)], axis=1)
    cos_tab = jnp.concatenate([cos_tab, ident], axis=0)
    sin_tab = jnp.concatenate([sin_tab, jnp.zeros((CTX_LEN, LANES), F32)], axis=0)
    return cos_tab, sin_tab


def _swap_half_lanes(w):
    perm = np.arange(QK_ROPE_DIM) ^ (QK_ROPE_DIM // 4)
    return w[..., perm]


def _mla_weights(w_dq, w_uq, w_dkv, w_ukv, w_o):
    uq = w_uq.reshape(Q_LORA_RANK, N_HEADS, QK_DIM)
    rope = uq[..., QK_NOPE_DIM:]
    uq = jnp.concatenate([uq[..., :QK_NOPE_DIM], rope, _swap_half_lanes(rope)], axis=-1)
    kpe = w_dkv[:, KV_LORA_RANK:]
    dkv = jnp.concatenate([w_dkv[:, :KV_LORA_RANK], kpe, _swap_half_lanes(kpe)], axis=-1)
    ukv = w_ukv.reshape(KV_LORA_RANK, N_HEADS, QK_NOPE_DIM + V_HEAD_DIM)
    uk_t = ukv[..., :QK_NOPE_DIM].reshape(KV_LORA_RANK, N_HEADS * QK_NOPE_DIM).T
    uv = ukv[..., QK_NOPE_DIM:].reshape(KV_LORA_RANK, N_HEADS * V_HEAD_DIM)
    return (w_dq.astype(BF16), uq.reshape(Q_LORA_RANK, N_HEADS * HEAD_PAD).astype(BF16),
            dkv.astype(BF16), uk_t.astype(BF16), uv.astype(BF16), w_o.astype(BF16))


def _rope(blk, cos_tab, sin_tab):
    return blk * cos_tab + pltpu.roll(blk, LANES // 2, axis=1) * sin_tab


def _mla_proj_kernel(x_ref, mod_ref, gpre_ref, wdq_ref, qn_ref, wuq_ref, wdkv_ref, kvn_ref,
                     wukt_ref, wuv_ref, cos_ref, sin_ref, q_ref, kt_ref, v_ref):
    h = _premix(x_ref[...], mod_ref[0], gpre_ref[...], 0).astype(BF16)
    cos_tab, sin_tab = cos_ref[...], sin_ref[...]
    q_lat = _rms(_dot(h, wdq_ref[...]), qn_ref[...]).astype(BF16)
    kv_a = _dot(h, wdkv_ref[...])
    c_kv = _rms(kv_a[:, :KV_LORA_RANK], kvn_ref[...]).astype(BF16)
    k_pe_t = _rope(kv_a[:, KV_LORA_RANK:], cos_tab, sin_tab).T.astype(BF16)
    k_nope_t = _qk(wukt_ref[...], c_kv)
    v_ref[...] = _dot(c_kv, wuv_ref[...]).astype(BF16)
    for hd in range(N_HEADS):
        lo = hd * HEAD_PAD
        q = _dot(q_lat, wuq_ref[:, lo:lo + HEAD_PAD])
        q_ref[:, lo:lo + LANES] = (q[:, :LANES] * ATTN_SCALE).astype(BF16)
        q_ref[:, lo + LANES:lo + HEAD_PAD] = (
            _rope(q[:, LANES:], cos_tab, sin_tab) * ATTN_SCALE).astype(BF16)
        kt_ref[lo:lo + LANES, :] = k_nope_t[hd * LANES:(hd + 1) * LANES].astype(BF16)
        kt_ref[lo + LANES:lo + HEAD_PAD, :] = k_pe_t


def _mla_proj(x, mods, g_pre, w_dq, q_norm, w_uq, w_dkv, kv_norm, w_uk_t, w_uv, cos_tab, sin_tab):
    tm = 256
    lat_tiles, seq_tiles = N_LAT // tm, SEQ // tm
    tab_spec = pl.BlockSpec(
        (tm, LANES), lambda i: (jnp.where(i < lat_tiles, i % seq_tiles, seq_tiles), 0))
    wide = N_HEADS * HEAD_PAD
    return pl.pallas_call(
        _mla_proj_kernel,
        grid=(N_TOK // tm,),
        in_specs=[
            _row_spec(tm, D_MODEL),
            _mod_spec(tm),
            _resident((1, D_MODEL)),
            _resident((D_MODEL, Q_LORA_RANK)),
            _resident((1, Q_LORA_RANK)),
            _resident((Q_LORA_RANK, wide)),
            _resident((D_MODEL, KV_LORA_RANK + LANES)),
            _resident((1, KV_LORA_RANK)),
            _resident((N_HEADS * QK_NOPE_DIM, KV_LORA_RANK)),
            _resident((KV_LORA_RANK, N_HEADS * V_HEAD_DIM)),
            tab_spec,
            tab_spec,
        ],
        out_specs=[_row_spec(tm, wide), pl.BlockSpec((wide, tm), lambda i: (0, i)),
                   _row_spec(tm, D_MODEL)],
        out_shape=[jax.ShapeDtypeStruct((N_TOK, wide), BF16),
                   jax.ShapeDtypeStruct((wide, N_TOK), BF16),
                   jax.ShapeDtypeStruct((N_TOK, D_MODEL), BF16)],
        compiler_params=_params(1),
        name="mla_proj",
    )(x, mods, g_pre, w_dq, q_norm, w_uq, w_dkv, kv_norm, w_uk_t, w_uv, cos_tab, sin_tab)


def _qk(q, k):
    return lax.dot_general(q, k, (((1,), (1,)), ((), ())), preferred_element_type=F32)


def _attn_kernel(tq, tk, q_ref, ktc_ref, ktl_ref, vc_ref, vl_ref, o_ref,
                 s_even, s_odd, m_even, m_odd, vaug_scr):
    vaug_scr[0:CTX_LEN, 0:V_HEAD_DIM] = vc_ref[...]
    vaug_scr[CTX_LEN:, 0:V_HEAD_DIM] = vl_ref[...]
    vaug_scr[:, V_HEAD_DIM:] = jnp.ones((CTX_LEN + SEQ, LANES), BF16)

    chunks = [(0, CTX_LEN)] + [(CTX_LEN + c * tk, tk) for c in range(SEQ // tk)]
    n_tiles = SEQ // tq

    def scores(tile, s_scr, m_scr):
        q = q_ref[pl.ds(pl.multiple_of(tile * tq, tq), tq), :]
        for n, (off, width) in enumerate(chunks):
            kt = ktc_ref[...] if n == 0 else ktl_ref[:, off - CTX_LEN:off - CTX_LEN + width]
            s = _dot(q, kt)
            s_scr[:, off:off + width] = s
            cm = s[:, 0:LANES]
            for t in range(1, width // LANES):
                cm = jnp.maximum(cm, s[:, t * LANES:(t + 1) * LANES])
            m_scr[...] = cm if n == 0 else jnp.maximum(m_scr[...], cm)

    def outputs(tile, s_scr, m_scr):
        m = jnp.max(m_scr[...], axis=1, keepdims=True)
        acc = None
        for off, width in chunks:
            p = jnp.exp(s_scr[:, off:off + width] - m).astype(BF16)
            part = _dot(p, vaug_scr[off:off + width, :])
            acc = part if acc is None else acc + part
        o_ref[pl.ds(pl.multiple_of(tile * tq, tq), tq), :] = (
            acc[:, :V_HEAD_DIM] / acc[:, V_HEAD_DIM:]).astype(BF16)

    scores(0, s_even, m_even)

    def tile_pair(j, carry):
        t = 2 * j
        scores(t + 1, s_odd, m_odd)
        outputs(t, s_even, m_even)
        scores(jnp.minimum(t + 2, n_tiles - 1), s_even, m_even)
        outputs(t + 1, s_odd, m_odd)
        return carry

    lax.fori_loop(0, n_tiles // 2, tile_pair, 0)


def _attention(q, kt, v, n_out_rows):
    tq, tk = 512, 256
    ctx_blk = N_LAT // CTX_LEN
    n_keys = CTX_LEN + SEQ
    return pl.pallas_call(
        functools.partial(_attn_kernel, tq, tk),
        grid=(BATCH, N_HEADS),
        in_specs=[
            pl.BlockSpec((SEQ, HEAD_PAD), lambda b, h: (b, h)),
            pl.BlockSpec((HEAD_PAD, CTX_LEN), lambda b, h: (h, ctx_blk + b)),
            pl.BlockSpec((HEAD_PAD, SEQ), lambda b, h: (h, b)),
            pl.BlockSpec((CTX_LEN, V_HEAD_DIM), lambda b, h: (ctx_blk + b, h)),
            pl.BlockSpec((SEQ, V_HEAD_DIM), lambda b, h: (b, h)),
        ],
        out_specs=pl.BlockSpec((SEQ, V_HEAD_DIM), lambda b, h: (b, h)),
        out_shape=jax.ShapeDtypeStruct((n_out_rows, N_HEADS * V_HEAD_DIM), BF16),
        scratch_shapes=[pltpu.VMEM((tq, n_keys), F32), pltpu.VMEM((tq, n_keys), F32),
                        pltpu.VMEM((tq, LANES), F32), pltpu.VMEM((tq, LANES), F32),
                        pltpu.VMEM((n_keys, V_HEAD_DIM + LANES), BF16)],
        compiler_params=_params(2),
        name="mla_attention",
    )(q, kt, kt, v, v)


def _attn_ctx_kernel(q_ref, kt_ref, v_ref, o_in_ref, o_ref):
    del o_in_ref
    s = _dot(q_ref[...], kt_ref[...])
    p = jnp.exp(s - jnp.max(s, axis=1, keepdims=True))
    l = jnp.sum(p, axis=1, keepdims=True)
    o_ref[...] = (_dot(p.astype(BF16), v_ref[...]) / l).astype(BF16)


def _attention_ctx(q, kt, v, o):
    ctx_blk = N_LAT // CTX_LEN
    q_spec = pl.BlockSpec((CTX_LEN, HEAD_PAD), lambda b, h: (ctx_blk + b, h))
    kt_spec = pl.BlockSpec((HEAD_PAD, CTX_LEN), lambda b, h: (h, ctx_blk + b))
    v_spec = pl.BlockSpec((CTX_LEN, V_HEAD_DIM), lambda b, h: (ctx_blk + b, h))
    return pl.pallas_call(
        _attn_ctx_kernel,
        grid=(BATCH, N_HEADS),
        in_specs=[q_spec, kt_spec, v_spec, pl.BlockSpec(memory_space=pl.ANY)],
        out_specs=v_spec,
        out_shape=jax.ShapeDtypeStruct(o.shape, o.dtype),
        input_output_aliases={3: 0},
        compiler_params=_params(2),
        name="mla_attention_ctx",
    )(q, kt, v, o)


def _conv_in_kernel(x_ref, mod_ref, gpre_ref, wa_ref, wg_ref, ba_ref, bg_ref, o_ref):
    h = _premix(x_ref[...], mod_ref[0], gpre_ref[...], 0).astype(BF16)
    a = _dot(h, wa_ref[...]) + ba_ref[...]
    gate = _dot(h, wg_ref[...]) + bg_ref[...]
    o_ref[...] = a * _sigmoid(gate)


def _conv_in(x, mods, g_pre, w_a, w_g, b_a, b_g):
    tm = 256
    return pl.pallas_call(
        _conv_in_kernel,
        grid=(N_TOK // tm,),
        in_specs=[
            _row_spec(tm, D_MODEL),
            _mod_spec(tm),
            _resident((1, D_MODEL)),
            _resident((D_MODEL, D_MODEL)),
            _resident((D_MODEL, D_MODEL)),
            _resident((1, D_MODEL)),
            _resident((1, D_MODEL)),
        ],
        out_specs=_row_spec(tm, D_MODEL),
        out_shape=jax.ShapeDtypeStruct((N_TOK, D_MODEL), F32),
        compiler_params=_params(1),
        name="conv_in_glu",
    )(x, mods, g_pre, w_a, w_g, b_a, b_g)


def _conv_out_kernel(tm, g_ref, left_ref, right_ref, wdw_ref, bdw_ref, lng_ref, lnb_ref, w2_ref,
                     b2_ref, x_ref, mod_ref, gpost_ref, o_ref, pad_scr, y_scr, shift_scr):
    i = pl.program_id(0)
    seg_tiles = SEQ // tm
    in_lat = i < N_LAT // tm
    has_left = jnp.logical_and(in_lat, i % seg_tiles != 0)
    has_right = jnp.logical_and(in_lat, i % seg_tiles != seg_tiles - 1)
    pad_scr[0:HALO, :] = jnp.where(has_left, left_ref[...], 0.0)
    pad_scr[HALO:HALO + tm, :] = g_ref[...]
    pad_scr[HALO + tm:HALO + tm + HALO, :] = jnp.where(has_right, right_ref[...], 0.0)

    rows, cols, sub, first, span = CONV_ROWS, LANES, SUBLANES, CONV_FIRST, CONV_SPAN
    for r0 in range(0, tm, rows):
        for c0 in range(0, D_MODEL, cols):
            acc = bdw_ref[:, c0:c0 + cols]
            for res in range(sub):
                if res:
                    shift_scr[res] = pad_scr[r0 + res:r0 + res + span, c0:c0 + cols]
                for off in range(res, first + CONV_WIDTH, sub):
                    if off >= first:
                        lo = off - res
                        tap = (shift_scr[res, lo:lo + rows, :] if res
                               else pad_scr[r0 + lo:r0 + lo + rows, c0:c0 + cols])
                        acc = acc + wdw_ref[off - first:off - first + 1, c0:c0 + cols] * tap
            y_scr[r0:r0 + rows, c0:c0 + cols] = acc

    y = _layer_norm(y_scr[...], lng_ref[...], lnb_ref[...])
    y = (y * _sigmoid(y)).astype(BF16)
    out = _dot(y, w2_ref[...]) + b2_ref[...]
    o_ref[...] = _post_residual(out, x_ref[...], mod_ref[0], gpost_ref[...], 2)


def _conv_out(g, w_dw, b_dw, ln_g, ln_b, w2, b2, x, mods, g_post, n_rows):
    tm = 256
    halo_per_tile = tm // HALO
    last_halo = N_TOK // HALO - 1
    return pl.pallas_call(
        functools.partial(_conv_out_kernel, tm),
        grid=(n_rows // tm,),
        in_specs=[
            _row_spec(tm, D_MODEL),
            pl.BlockSpec((HALO, D_MODEL), lambda i: (jnp.maximum(i * halo_per_tile - 1, 0), 0)),
            pl.BlockSpec((HALO, D_MODEL),
                         lambda i: (jnp.minimum((i + 1) * halo_per_tile, last_halo), 0)),
            _resident((CONV_WIDTH, D_MODEL)),
            _resident((1, D_MODEL)),
            _resident((1, D_MODEL)),
            _resident((1, D_MODEL)),
            _resident((D_MODEL, D_MODEL)),
            _resident((1, D_MODEL)),
            _row_spec(tm, D_MODEL),
            _mod_spec(tm),
            _resident((1, D_MODEL)),
        ],
        out_specs=_row_spec(tm, D_MODEL),
        out_shape=jax.ShapeDtypeStruct((n_rows, D_MODEL), F32),
        scratch_shapes=[pltpu.VMEM((tm + 2 * HALO, D_MODEL), F32), pltpu.VMEM((tm, D_MODEL), F32),
                        pltpu.VMEM((SUBLANES, CONV_SPAN, LANES), F32)],
        compiler_params=_params(1),
        name="conv_dw_out",
    )(g, g, g, w_dw, b_dw, ln_g, ln_b, w2, b2, x, mods, g_post)


def _gmlp_in_kernel(normalize, x_ref, mod_ref, gpre_ref, w_ref, b_ref, lng_ref, lnb_ref, o_ref):
    h = _premix(x_ref[...], mod_ref[0], gpre_ref[...], 0).astype(BF16)
    a = _gelu(_dot(h, w_ref[...]) + b_ref[...])
    if normalize:
        a = _layer_norm(a, lng_ref[...], lnb_ref[...])
    o_ref[...] = a.astype(o_ref.dtype)


def _gmlp_in(x, mods, g_pre, w, b, ln_g, ln_b, normalize, out_dtype):
    tm = 256
    return pl.pallas_call(
        functools.partial(_gmlp_in_kernel, normalize),
        grid=(N_TOK // tm,),
        in_specs=[
            _row_spec(tm, D_MODEL),
            _mod_spec(tm),
            _resident((1, D_MODEL)),
            _resident((D_MODEL, D_MODEL)),
            _resident((1, D_MODEL)),
            _resident((1, D_MODEL)),
            _resident((1, D_MODEL)),
        ],
        out_specs=_row_spec(tm, D_MODEL),
        out_shape=jax.ShapeDtypeStruct((N_TOK, D_MODEL), out_dtype),
        compiler_params=_params(1),
        name="gmlp_in",
    )(x, mods, g_pre, w, b, ln_g, ln_b)


def _gmlp_out_kernel(tm, u_ref, v_ref, ws_ref, bs_ref, w_ref, b_ref, x_ref, mod_ref, gpost_ref,
                     o_ref, t_scr):
    for g in range(N_GROUPS):
        c0 = g * GROUP_DIM
        w_s = ws_ref[g]
        for r0 in range(0, tm, CHUNK):
            sv = _dot(w_s, v_ref[r0:r0 + CHUNK, c0:c0 + GROUP_DIM]) + bs_ref[:, c0:c0 + GROUP_DIM]
            t_scr[r0:r0 + CHUNK, c0:c0 + GROUP_DIM] = (
                u_ref[r0:r0 + CHUNK, c0:c0 + GROUP_DIM] * sv).astype(BF16)
    out = _dot(t_scr[...], w_ref[...]) + b_ref[...]
    o_ref[...] = _post_residual(out, x_ref[...], mod_ref[0], gpost_ref[...], 2)


def _gmlp_out(u, v, w_s, b_s_wide, w, b, x, mods, g_post, n_rows):
    tm = 256
    return pl.pallas_call(
        functools.partial(_gmlp_out_kernel, tm),
        grid=(n_rows // tm,),
        in_specs=[
            _row_spec(tm, D_MODEL),
            _row_spec(tm, D_MODEL),
            _resident((N_GROUPS, CHUNK, CHUNK)),
            _resident((CHUNK, D_MODEL)),
            _resident((D_MODEL, D_MODEL)),
            _resident((1, D_MODEL)),
            _row_spec(tm, D_MODEL),
            _mod_spec(tm),
            _resident((1, D_MODEL)),
        ],
        out_specs=_row_spec(tm, D_MODEL),
        out_shape=jax.ShapeDtypeStruct((n_rows, D_MODEL), F32),
        scratch_shapes=[pltpu.VMEM((tm, D_MODEL), BF16)],
        compiler_params=_params(1),
        name="gmlp_spatial_out",
    )(u, v, w_s, b_s_wide, w, b, x, mods, g_post)


def _row(v):
    return v.reshape(1, -1)


def kernel(x, c, ctx, c_ctx, ada_w, ada_b, norm_mix_pre, norm_mix_post, norm_ffn_pre, norm_ffn_post, mla_w_dq, mla_q_norm, mla_w_uq, mla_w_dkv, mla_kv_norm, mla_w_ukv, mla_w_o, conv_w_pw1, conv_b_pw1, conv_w_dw, conv_b_dw, conv_ln_g, conv_ln_b, conv_w_pw2, conv_b_pw2, gmlp_w_uv, gmlp_b_uv, gmlp_ln_g, gmlp_ln_b, gmlp_w_s, gmlp_b_s, gmlp_w_out, gmlp_b_out, ffn_w1, ffn_w2):
    stream = jnp.concatenate([x.reshape(N_LAT, D_MODEL), ctx.reshape(N_CTX, D_MODEL)], axis=0)
    cond = jnp.concatenate([c, c_ctx[None, :], jnp.zeros((8 - BATCH - 1, D_MODEL), F32)], axis=0)
    mods_all = _ada_all(cond, ada_w, ada_b)[:, :N_SETS].reshape(DEPTH, N_SETS, N_MOD, D_MODEL)
    cos_tab, sin_tab = _rope_tables()
    zero_bias = jnp.zeros((1, D_MODEL), F32)

    for i in range(DEPTH):
        last = i == DEPTH - 1
        kind, j = i % N_MIXERS, i // N_MIXERS
        n_rows = N_LAT if last else N_TOK
        mods = mods_all[i]
        if kind == 0:
            w_dq, w_uq, w_dkv, w_uk_t, w_uv, w_o = _mla_weights(
                mla_w_dq[j], mla_w_uq[j], mla_w_dkv[j], mla_w_ukv[j], mla_w_o[j])
            q, kt, v = _mla_proj(stream, mods, _row(norm_mix_pre[i]), w_dq, _row(mla_q_norm[j]),
                                 w_uq, w_dkv, _row(mla_kv_norm[j]), w_uk_t, w_uv, cos_tab, sin_tab)
            o = _attention(q, kt, v, n_rows)
            if not last:
                o = _attention_ctx(q, kt, v, o)
            stream = _mix_out(o, w_o, zero_bias, stream, mods, _row(norm_mix_post[i]), n_rows)
        elif kind == 1:
            w1 = conv_w_pw1[j].astype(BF16)
            g = _conv_in(stream, mods, _row(norm_mix_pre[i]), w1[:, :D_MODEL], w1[:, D_MODEL:],
                         _row(conv_b_pw1[j, :D_MODEL]), _row(conv_b_pw1[j, D_MODEL:]))
            stream = _conv_out(g, conv_w_dw[j], _row(conv_b_dw[j]), _row(conv_ln_g[j]),
                               _row(conv_ln_b[j]), conv_w_pw2[j].astype(BF16),
                               _row(conv_b_pw2[j]), stream, mods, _row(norm_mix_post[i]), n_rows)
        else:
            w_uv = gmlp_w_uv[j].astype(BF16)
            ln_g, ln_b = _row(gmlp_ln_g[j]), _row(gmlp_ln_b[j])
            u = _gmlp_in(stream, mods, _row(norm_mix_pre[i]), w_uv[:, :D_MODEL],
                         _row(gmlp_b_uv[j, :D_MODEL]), ln_g, ln_b, False, F32)
            v = _gmlp_in(stream, mods, _row(norm_mix_pre[i]), w_uv[:, D_MODEL:],
                         _row(gmlp_b_uv[j, D_MODEL:]), ln_g, ln_b, True, BF16)
            b_s_wide = jnp.repeat(gmlp_b_s[j].T, GROUP_DIM, axis=1)
            stream = _gmlp_out(u, v, gmlp_w_s[j].astype(BF16), b_s_wide,
                               gmlp_w_out[j].astype(BF16), _row(gmlp_b_out[j]), stream, mods,
                               _row(norm_mix_post[i]), n_rows)
        stream = _ffn(stream, mods, _row(norm_ffn_pre[i]), _row(norm_ffn_post[i]),
                      ffn_w1[i].astype(BF16), ffn_w2[i].astype(BF16), n_rows)
    return stream.reshape(BATCH, SEQ, D_MODEL)
```

```python
import functools

import jax
import jax.numpy as jnp
import numpy as np
from jax import lax
from jax.experimental import pallas as pl
from jax.experimental.pallas import tpu as pltpu

D_MODEL = 2048
BATCH = 2
SEQ = 4096
DEPTH = 4
CTX_LEN = 256
GRID_W = 64
N_MIXERS = 3
N_HEADS = 16
QK_NOPE_DIM = 128
QK_ROPE_DIM = 64
QK_DIM = QK_NOPE_DIM + QK_ROPE_DIM
V_HEAD_DIM = 128
Q_LORA_RANK = 768
KV_LORA_RANK = 512
ROPE_THETA = 10000.0
ATTN_SCALE = QK_DIM ** -0.5
CONV_WIDTH = 31
CONV_PAD = CONV_WIDTH // 2
CHUNK = 128
N_GROUPS = 16
GROUP_DIM = D_MODEL // N_GROUPS
D_FF = 4 * D_MODEL
N_MOD = 6
EPS = 1e-6

N_LAT = BATCH * SEQ
N_CTX = BATCH * CTX_LEN
N_TOK = N_LAT + N_CTX
N_SETS = 3
HEAD_PAD = 256
LANES = 128
SUBLANES = 8
HALO = 16
CONV_ROWS = 128
CONV_FIRST = HALO - CONV_PAD
CONV_SPAN = CONV_ROWS + SUBLANES * ((CONV_FIRST + CONV_WIDTH - 1) // SUBLANES)
VMEM_LIMIT = 56 * 1024 * 1024

F32 = jnp.float32
BF16 = jnp.bfloat16


def _dot(a, b):
    return jnp.dot(a, b, preferred_element_type=F32)


def _rms(x, g):
    return x * lax.rsqrt(jnp.mean(x * x, axis=-1, keepdims=True) + EPS) * g


def _layer_norm(x, g, b):
    mu = jnp.mean(x, axis=-1, keepdims=True)
    xc = x - mu
    var = jnp.mean(xc * xc, axis=-1, keepdims=True)
    return xc * lax.rsqrt(var + EPS) * g + b


def _sigmoid(x):
    return 1.0 / (1.0 + jnp.exp(-x))


def _gelu(x):
    return 0.5 * x * (1.0 + lax.erf(x * np.float32(np.sqrt(0.5))))


def _premix(x, mod, g, shift_idx):
    return _rms(x, g) * (1.0 + mod[shift_idx + 1:shift_idx + 2]) + mod[shift_idx:shift_idx + 1]


def _params(n_grid_axes):
    return pltpu.CompilerParams(
        dimension_semantics=("arbitrary",) * n_grid_axes,
        vmem_limit_bytes=VMEM_LIMIT)


def _resident(shape):
    nd = len(shape)
    return pl.BlockSpec(shape, lambda *_: (0,) * nd, pipeline_mode=pl.Buffered(1))


def _mod_spec(tm):
    return pl.BlockSpec((1, N_MOD, D_MODEL),
                        lambda i, *_: (jnp.minimum(i * tm // SEQ, N_SETS - 1), 0, 0))


def _row_spec(tm, width):
    return pl.BlockSpec((tm, width), lambda i, *_: (i, 0))


def _ada_kernel(cond_ref, w_ref, b_ref, o_ref):
    c = cond_ref[...]
    s = (c * _sigmoid(c)).astype(BF16)
    o_ref[0] = _dot(s, w_ref[0].astype(BF16)) + b_ref[0]


def _ada_all(cond, ada_w, ada_b):
    tn = 1024
    n_out = N_MOD * D_MODEL
    return pl.pallas_call(
        _ada_kernel,
        grid=(DEPTH, n_out // tn),
        in_specs=[
            pl.BlockSpec((8, D_MODEL), lambda l, j: (0, 0)),
            pl.BlockSpec((1, D_MODEL, tn), lambda l, j: (l, 0, j)),
            pl.BlockSpec((1, 1, tn), lambda l, j: (l, 0, j)),
        ],
        out_specs=pl.BlockSpec((1, 8, tn), lambda l, j: (l, 0, j)),
        out_shape=jax.ShapeDtypeStruct((DEPTH, 8, n_out), F32),
        compiler_params=_params(2),
        name="ada_mod",
    )(cond, ada_w, ada_b.reshape(DEPTH, 1, n_out))


def _post_residual(y, x, mod, g_post, gate_idx):
    return x + mod[gate_idx:gate_idx + 1] * _rms(y, g_post)


def _ffn_kernel(n_f, x_ref, mod_ref, gpre_ref, gpost_ref, w1_ref, w2_ref, o_ref, h_scr, acc_scr):
    f = pl.program_id(1)

    @pl.when(f == 0)
    def _():
        h_scr[...] = _premix(x_ref[...], mod_ref[0], gpre_ref[...], 3).astype(BF16)
        acc_scr[...] = jnp.zeros(acc_scr.shape, F32)

    a = _dot(h_scr[...], w1_ref[0])
    a = jnp.square(jnp.maximum(a, 0.0)).astype(BF16)
    acc_scr[...] += _dot(a, w2_ref[0])

    @pl.when(f == n_f - 1)
    def _():
        o_ref[...] = _post_residual(acc_scr[...], x_ref[...], mod_ref[0], gpost_ref[...], 5)


def _ffn(x, mods, g_pre, g_post, w1, w2, layer, n_rows):
    tm, tf = 512, 1024
    n_f = D_FF // tf
    return pl.pallas_call(
        functools.partial(_ffn_kernel, n_f),
        grid=(n_rows // tm, n_f),
        in_specs=[
            _row_spec(tm, D_MODEL),
            _mod_spec(tm),
            pl.BlockSpec((1, D_MODEL), lambda i, f: (0, 0)),
            pl.BlockSpec((1, D_MODEL), lambda i, f: (0, 0)),
            pl.BlockSpec((1, D_MODEL, tf), lambda i, f: (layer, 0, f)),
            pl.BlockSpec((1, tf, D_MODEL), lambda i, f: (layer, f, 0)),
        ],
        out_specs=_row_spec(tm, D_MODEL),
        out_shape=jax.ShapeDtypeStruct((n_rows, D_MODEL), F32),
        scratch_shapes=[pltpu.VMEM((tm, D_MODEL), BF16), pltpu.VMEM((tm, D_MODEL), F32)],
        compiler_params=_params(2),
        name="ffn",
    )(x, mods, g_pre, g_post, w1, w2)


def _mix_out_kernel(a_ref, w_ref, b_ref, x_ref, mod_ref, gpost_ref, o_ref):
    y = _dot(a_ref[...], w_ref[...]) + b_ref[...]
    o_ref[...] = _post_residual(y, x_ref[...], mod_ref[0], gpost_ref[...], 2)


def _mix_out(a, w, b, x, mods, g_post, n_rows):
    tm = 256
    return pl.pallas_call(
        _mix_out_kernel,
        grid=(n_rows // tm,),
        in_specs=[
            _row_spec(tm, D_MODEL),
            _resident((D_MODEL, D_MODEL)),
            _resident((1, D_MODEL)),
            _row_spec(tm, D_MODEL),
            _mod_spec(tm),
            _resident((1, D_MODEL)),
        ],
        out_specs=_row_spec(tm, D_MODEL),
        out_shape=jax.ShapeDtypeStruct((n_rows, D_MODEL), F32),
        compiler_params=_params(1),
        name="mix_out",
    )(a, w, b, x, mods, g_post)


def _rope_tables():
    t = jnp.arange(SEQ, dtype=jnp.int32)
    row = (t // GRID_W).astype(F32)
    col = (t % GRID_W).astype(F32)
    half = QK_ROPE_DIM // 2
    inv = ROPE_THETA ** (-jnp.arange(0, half, 2, dtype=F32) / half)
    ang_r, ang_c = row[:, None] * inv, col[:, None] * inv
    cr, sr, cc, sc = jnp.cos(ang_r), jnp.sin(ang_r), jnp.cos(ang_c), jnp.sin(ang_c)
    zeros = jnp.zeros((SEQ, QK_ROPE_DIM), F32)
    cos_tab = jnp.concatenate([cr, cr, cc, cc, zeros], axis=1)
    sin_tab = jnp.concatenate([-sr, sr, -sc, sc, zeros], axis=1)
    ident = jnp.concatenate([jnp.ones((CTX_LEN, QK_ROPE_DIM), F32),
                             jnp.zeros((CTX_LEN, QK_ROPE_DIM), F32)], axis=1)
    cos_tab = jnp.concatenate([cos_tab, ident], axis=0)
    sin_tab = jnp.concatenate([sin_tab, jnp.zeros((CTX_LEN, LANES), F32)], axis=0)
    return cos_tab, sin_tab


def _swap_half_lanes(w):
    perm = np.arange(QK_ROPE_DIM) ^ (QK_ROPE_DIM // 4)
    return w[..., perm]


def _mla_weights(w_dq, w_uq, w_dkv, w_ukv, w_o):
    uq = w_uq.reshape(Q_LORA_RANK, N_HEADS, QK_DIM)
    rope = uq[..., QK_NOPE_DIM:]
    uq = jnp.concatenate([uq[..., :QK_NOPE_DIM], rope, _swap_half_lanes(rope)], axis=-1)
    kpe = w_dkv[:, KV_LORA_RANK:]
    dkv = jnp.concatenate([w_dkv[:, :KV_LORA_RANK], kpe, _swap_half_lanes(kpe)], axis=-1)
    ukv = w_ukv.reshape(KV_LORA_RANK, N_HEADS, QK_NOPE_DIM + V_HEAD_DIM)
    uk_t = ukv[..., :QK_NOPE_DIM].reshape(KV_LORA_RANK, N_HEADS * QK_NOPE_DIM).T
    uv = ukv[..., QK_NOPE_DIM:].reshape(KV_LORA_RANK, N_HEADS * V_HEAD_DIM)
    return (w_dq.astype(BF16), uq.reshape(Q_LORA_RANK, N_HEADS * HEAD_PAD).astype(BF16),
            dkv.astype(BF16), uk_t.astype(BF16), uv.astype(BF16), w_o.astype(BF16))


def _rope(blk, cos_tab, sin_tab):
    return blk * cos_tab + pltpu.roll(blk, LANES // 2, axis=1) * sin_tab


def _mla_proj_kernel(x_ref, mod_ref, gpre_ref, wdq_ref, qn_ref, wuq_ref, wdkv_ref, kvn_ref,
                     wukt_ref, wuv_ref, cos_ref, sin_ref, q_ref, kt_ref, v_ref):
    h = _premix(x_ref[...], mod_ref[0], gpre_ref[...], 0).astype(BF16)
    cos_tab, sin_tab = cos_ref[...], sin_ref[...]
    q_lat = _rms(_dot(h, wdq_ref[...]), qn_ref[...]).astype(BF16)
    kv_a = _dot(h, wdkv_ref[...])
    c_kv = _rms(kv_a[:, :KV_LORA_RANK], kvn_ref[...]).astype(BF16)
    k_pe_t = _rope(kv_a[:, KV_LORA_RANK:], cos_tab, sin_tab).T.astype(BF16)
    k_nope_t = _qk(wukt_ref[...], c_kv)
    v_ref[...] = _dot(c_kv, wuv_ref[...]).astype(BF16)
    for hd in range(N_HEADS):
        lo = hd * HEAD_PAD
        q = _dot(q_lat, wuq_ref[:, lo:lo + HEAD_PAD])
        q_ref[:, lo:lo + LANES] = (q[:, :LANES] * ATTN_SCALE).astype(BF16)
        q_ref[:, lo + LANES:lo + HEAD_PAD] = (
            _rope(q[:, LANES:], cos_tab, sin_tab) * ATTN_SCALE).astype(BF16)
        kt_ref[lo:lo + LANES, :] = k_nope_t[hd * LANES:(hd + 1) * LANES].astype(BF16)
        kt_ref[lo + LANES:lo + HEAD_PAD, :] = k_pe_t


def _mla_proj(x, mods, g_pre, w_dq, q_norm, w_uq, w_dkv, kv_norm, w_uk_t, w_uv, cos_tab, sin_tab):
    tm = 256
    lat_tiles, seq_tiles = N_LAT // tm, SEQ // tm
    tab_spec = pl.BlockSpec(
        (tm, LANES), lambda i: (jnp.where(i < lat_tiles, i % seq_tiles, seq_tiles), 0))
    wide = N_HEADS * HEAD_PAD
    return pl.pallas_call(
        _mla_proj_kernel,
        grid=(N_TOK // tm,),
        in_specs=[
            _row_spec(tm, D_MODEL),
            _mod_spec(tm),
            _resident((1, D_MODEL)),
            _resident((D_MODEL, Q_LORA_RANK)),
            _resident((1, Q_LORA_RANK)),
            _resident((Q_LORA_RANK, wide)),
            _resident((D_MODEL, KV_LORA_RANK + LANES)),
            _resident((1, KV_LORA_RANK)),
            _resident((N_HEADS * QK_NOPE_DIM, KV_LORA_RANK)),
            _resident((KV_LORA_RANK, N_HEADS * V_HEAD_DIM)),
            tab_spec,
            tab_spec,
        ],
        out_specs=[_row_spec(tm, wide), pl.BlockSpec((wide, tm), lambda i: (0, i)),
                   _row_spec(tm, D_MODEL)],
        out_shape=[jax.ShapeDtypeStruct((N_TOK, wide), BF16),
                   jax.ShapeDtypeStruct((wide, N_TOK), BF16),
                   jax.ShapeDtypeStruct((N_TOK, D_MODEL), BF16)],
        compiler_params=_params(1),
        name="mla_proj",
    )(x, mods, g_pre, w_dq, q_norm, w_uq, w_dkv, kv_norm, w_uk_t, w_uv, cos_tab, sin_tab)


def _qk(q, k):
    return lax.dot_general(q, k, (((1,), (1,)), ((), ())), preferred_element_type=F32)


def _attn_kernel(tq, tk, q_ref, ktc_ref, ktl_ref, vc_ref, vl_ref, qn_ref, ktcn_ref, ktln_ref, o_ref,
                 s_even, s_odd, m_even, m_odd, vaug_scr):
    vaug_scr[0:CTX_LEN, 0:V_HEAD_DIM] = vc_ref[...]
    vaug_scr[CTX_LEN:, 0:V_HEAD_DIM] = vl_ref[...]
    vaug_scr[:, V_HEAD_DIM:] = jnp.ones((CTX_LEN + SEQ, LANES), BF16)

    chunks = [(0, CTX_LEN)] + [(CTX_LEN + c * tk, tk) for c in range(SEQ // tk)]
    n_tiles = SEQ // tq

    def rows(tile):
        start = tile * tq if isinstance(tile, int) else pl.multiple_of(tile * tq, tq)
        return pl.ds(start, tq)

    def scores(q, kc_ref, kl_ref, s_scr, m_scr):
        for n, (off, width) in enumerate(chunks):
            kt = kc_ref[...] if n == 0 else kl_ref[:, off - CTX_LEN:off - CTX_LEN + width]
            s = _dot(q, kt)
            s_scr[:, off:off + width] = s
            cm = s[:, 0:LANES]
            for t in range(1, width // LANES):
                cm = jnp.maximum(cm, s[:, t * LANES:(t + 1) * LANES])
            m_scr[...] = cm if n == 0 else jnp.maximum(m_scr[...], cm)

    def outputs(tile, s_scr, m_scr):
        m = jnp.max(m_scr[...], axis=1, keepdims=True)
        acc = None
        for off, width in chunks:
            p = jnp.exp(s_scr[:, off:off + width] - m).astype(BF16)
            part = _dot(p, vaug_scr[off:off + width, :])
            acc = part if acc is None else acc + part
        o_ref[rows(tile), :] = (acc[:, :V_HEAD_DIM] / acc[:, V_HEAD_DIM:]).astype(BF16)

    def own_scores(tile, s_scr, m_scr):
        scores(q_ref[rows(tile), :], ktc_ref, ktl_ref, s_scr, m_scr)

    @pl.when(pl.program_id(0) == 0)
    def _():
        own_scores(0, s_even, m_even)

    def tile_pair(j, carry):
        t = 2 * j
        own_scores(t + 1, s_odd, m_odd)
        outputs(t, s_even, m_even)
        own_scores(t + 2, s_even, m_even)
        outputs(t + 1, s_odd, m_odd)
        return carry

    lax.fori_loop(0, n_tiles // 2 - 1, tile_pair, 0)
    own_scores(n_tiles - 1, s_odd, m_odd)
    outputs(n_tiles - 2, s_even, m_even)
    scores(qn_ref[...], ktcn_ref, ktln_ref, s_even, m_even)
    outputs(n_tiles - 1, s_odd, m_odd)


def _attention(q, kt, v, n_out_rows):
    tq, tk = 512, 256
    ctx_blk = N_LAT // CTX_LEN
    n_keys = CTX_LEN + SEQ
    n_steps = BATCH * N_HEADS

    def this(s):
        return s // N_HEADS, s % N_HEADS

    def following(s):
        return this(jnp.minimum(s + 1, n_steps - 1))

    def spec(shape, pick, index):
        return pl.BlockSpec(shape, lambda s: index(*pick(s)))

    return pl.pallas_call(
        functools.partial(_attn_kernel, tq, tk),
        grid=(n_steps,),
        in_specs=[
            spec((SEQ, HEAD_PAD), this, lambda b, h: (b, h)),
            spec((HEAD_PAD, CTX_LEN), this, lambda b, h: (h, ctx_blk + b)),
            spec((HEAD_PAD, SEQ), this, lambda b, h: (h, b)),
            spec((CTX_LEN, V_HEAD_DIM), this, lambda b, h: (ctx_blk + b, h)),
            spec((SEQ, V_HEAD_DIM), this, lambda b, h: (b, h)),
            spec((tq, HEAD_PAD), following, lambda b, h: (b * (SEQ // tq), h)),
            spec((HEAD_PAD, CTX_LEN), following, lambda b, h: (h, ctx_blk + b)),
            spec((HEAD_PAD, SEQ), following, lambda b, h: (h, b)),
        ],
        out_specs=spec((SEQ, V_HEAD_DIM), this, lambda b, h: (b, h)),
        out_shape=jax.ShapeDtypeStruct((n_out_rows, N_HEADS * V_HEAD_DIM), BF16),
        scratch_shapes=[pltpu.VMEM((tq, n_keys), F32), pltpu.VMEM((tq, n_keys), F32),
                        pltpu.VMEM((tq, LANES), F32), pltpu.VMEM((tq, LANES), F32),
                        pltpu.VMEM((n_keys, V_HEAD_DIM + LANES), BF16)],
        compiler_params=_params(1),
        name="mla_attention",
    )(q, kt, kt, v, v, q, kt, kt)


def _attn_ctx_kernel(q_ref, kt_ref, v_ref, o_in_ref, o_ref):
    del o_in_ref
    s = _dot(q_ref[...], kt_ref[...])
    p = jnp.exp(s - jnp.max(s, axis=1, keepdims=True))
    l = jnp.sum(p, axis=1, keepdims=True)
    o_ref[...] = (_dot(p.astype(BF16), v_ref[...]) / l).astype(BF16)


def _attention_ctx(q, kt, v, o):
    ctx_blk = N_LAT // CTX_LEN
    q_spec = pl.BlockSpec((CTX_LEN, HEAD_PAD), lambda b, h: (ctx_blk + b, h))
    kt_spec = pl.BlockSpec((HEAD_PAD, CTX_LEN), lambda b, h: (h, ctx_blk + b))
    v_spec = pl.BlockSpec((CTX_LEN, V_HEAD_DIM), lambda b, h: (ctx_blk + b, h))
    return pl.pallas_call(
        _attn_ctx_kernel,
        grid=(BATCH, N_HEADS),
        in_specs=[q_spec, kt_spec, v_spec, pl.BlockSpec(memory_space=pl.ANY)],
        out_specs=v_spec,
        out_shape=jax.ShapeDtypeStruct(o.shape, o.dtype),
        input_output_aliases={3: 0},
        compiler_params=_params(2),
        name="mla_attention_ctx",
    )(q, kt, v, o)


def _conv_in_kernel(x_ref, mod_ref, gpre_ref, w_ref, b_ref, o_ref):
    h = _premix(x_ref[...], mod_ref[0], gpre_ref[...], 0).astype(BF16)
    a = _dot(h, w_ref[:, :D_MODEL]) + b_ref[:, :D_MODEL]
    gate = _dot(h, w_ref[:, D_MODEL:]) + b_ref[:, D_MODEL:]
    o_ref[...] = a * _sigmoid(gate)


def _conv_in(x, mods, g_pre, w, b):
    tm = 256
    return pl.pallas_call(
        _conv_in_kernel,
        grid=(N_TOK // tm,),
        in_specs=[
            _row_spec(tm, D_MODEL),
            _mod_spec(tm),
            _resident((1, D_MODEL)),
            _resident((D_MODEL, 2 * D_MODEL)),
            _resident((1, 2 * D_MODEL)),
        ],
        out_specs=_row_spec(tm, D_MODEL),
        out_shape=jax.ShapeDtypeStruct((N_TOK, D_MODEL), F32),
        compiler_params=_params(1),
        name="conv_in_glu",
    )(x, mods, g_pre, w, b)


def _conv_out_kernel(tm, g_ref, left_ref, right_ref, wdw_ref, bdw_ref, lng_ref, lnb_ref, w2_ref,
                     b2_ref, x_ref, mod_ref, gpost_ref, o_ref, pad_scr, y_scr, shift_scr):
    i = pl.program_id(0)
    seg_tiles = SEQ // tm
    in_lat = i < N_LAT // tm
    has_left = jnp.logical_and(in_lat, i % seg_tiles != 0)
    has_right = jnp.logical_and(in_lat, i % seg_tiles != seg_tiles - 1)
    pad_scr[0:HALO, :] = jnp.where(has_left, left_ref[...], 0.0)
    pad_scr[HALO:HALO + tm, :] = g_ref[...]
    pad_scr[HALO + tm:HALO + tm + HALO, :] = jnp.where(has_right, right_ref[...], 0.0)

    rows, cols, sub, first, span = CONV_ROWS, LANES, SUBLANES, CONV_FIRST, CONV_SPAN
    for r0 in range(0, tm, rows):
        for c0 in range(0, D_MODEL, cols):
            acc = bdw_ref[:, c0:c0 + cols]
            for res in range(sub):
                if res:
                    shift_scr[res] = pad_scr[r0 + res:r0 + res + span, c0:c0 + cols]
                for off in range(res, first + CONV_WIDTH, sub):
                    if off >= first:
                        lo = off - res
                        tap = (shift_scr[res, lo:lo + rows, :] if res
                               else pad_scr[r0 + lo:r0 + lo + rows, c0:c0 + cols])
                        acc = acc + wdw_ref[off - first:off - first + 1, c0:c0 + cols] * tap
            y_scr[r0:r0 + rows, c0:c0 + cols] = acc

    y = _layer_norm(y_scr[...], lng_ref[...], lnb_ref[...])
    y = (y * _sigmoid(y)).astype(BF16)
    out = _dot(y, w2_ref[...]) + b2_ref[...]
    o_ref[...] = _post_residual(out, x_ref[...], mod_ref[0], gpost_ref[...], 2)


def _conv_out(g, w_dw, b_dw, ln_g, ln_b, w2, b2, x, mods, g_post, n_rows):
    tm = 256
    halo_per_tile = tm // HALO
    last_halo = N_TOK // HALO - 1
    return pl.pallas_call(
        functools.partial(_conv_out_kernel, tm),
        grid=(n_rows // tm,),
        in_specs=[
            _row_spec(tm, D_MODEL),
            pl.BlockSpec((HALO, D_MODEL), lambda i: (jnp.maximum(i * halo_per_tile - 1, 0), 0)),
            pl.BlockSpec((HALO, D_MODEL),
                         lambda i: (jnp.minimum((i + 1) * halo_per_tile, last_halo), 0)),
            _resident((CONV_WIDTH, D_MODEL)),
            _resident((1, D_MODEL)),
            _resident((1, D_MODEL)),
            _resident((1, D_MODEL)),
            _resident((D_MODEL, D_MODEL)),
            _resident((1, D_MODEL)),
            _row_spec(tm, D_MODEL),
            _mod_spec(tm),
            _resident((1, D_MODEL)),
        ],
        out_specs=_row_spec(tm, D_MODEL),
        out_shape=jax.ShapeDtypeStruct((n_rows, D_MODEL), F32),
        scratch_shapes=[pltpu.VMEM((tm + 2 * HALO, D_MODEL), F32), pltpu.VMEM((tm, D_MODEL), F32),
                        pltpu.VMEM((SUBLANES, CONV_SPAN, LANES), F32)],
        compiler_params=_params(1),
        name="conv_dw_out",
    )(g, g, g, w_dw, b_dw, ln_g, ln_b, w2, b2, x, mods, g_post)


def _gmlp_in_kernel(x_ref, mod_ref, gpre_ref, w_ref, b_ref, lng_ref, lnb_ref, u_ref, v_ref):
    h = _premix(x_ref[...], mod_ref[0], gpre_ref[...], 0).astype(BF16)
    u_ref[...] = _gelu(_dot(h, w_ref[:, :D_MODEL]) + b_ref[:, :D_MODEL])
    v = _gelu(_dot(h, w_ref[:, D_MODEL:]) + b_ref[:, D_MODEL:])
    v_ref[...] = _layer_norm(v, lng_ref[...], lnb_ref[...]).astype(BF16)


def _gmlp_in(x, mods, g_pre, w_uv, b_uv, ln_g, ln_b):
    tm = 256
    return pl.pallas_call(
        _gmlp_in_kernel,
        grid=(N_TOK // tm,),
        in_specs=[
            _row_spec(tm, D_MODEL),
            _mod_spec(tm),
            _resident((1, D_MODEL)),
            _resident((D_MODEL, 2 * D_MODEL)),
            _resident((1, 2 * D_MODEL)),
            _resident((1, D_MODEL)),
            _resident((1, D_MODEL)),
        ],
        out_specs=[_row_spec(tm, D_MODEL), _row_spec(tm, D_MODEL)],
        out_shape=[jax.ShapeDtypeStruct((N_TOK, D_MODEL), F32),
                   jax.ShapeDtypeStruct((N_TOK, D_MODEL), BF16)],
        compiler_params=_params(1),
        name="gmlp_in",
    )(x, mods, g_pre, w_uv, b_uv, ln_g, ln_b)


def _gmlp_out_kernel(tm, u_ref, v_ref, ws_ref, bs_ref, w_ref, b_ref, x_ref, mod_ref, gpost_ref,
                     o_ref, t_scr):
    for g in range(N_GROUPS):
        c0 = g * GROUP_DIM
        w_s = ws_ref[g]
        for r0 in range(0, tm, CHUNK):
            sv = _dot(w_s, v_ref[r0:r0 + CHUNK, c0:c0 + GROUP_DIM]) + bs_ref[:, c0:c0 + GROUP_DIM]
            t_scr[r0:r0 + CHUNK, c0:c0 + GROUP_DIM] = (
                u_ref[r0:r0 + CHUNK, c0:c0 + GROUP_DIM] * sv).astype(BF16)
    out = _dot(t_scr[...], w_ref[...]) + b_ref[...]
    o_ref[...] = _post_residual(out, x_ref[...], mod_ref[0], gpost_ref[...], 2)


def _gmlp_out(u, v, w_s, b_s_wide, w, b, x, mods, g_post, n_rows):
    tm = 256
    return pl.pallas_call(
        functools.partial(_gmlp_out_kernel, tm),
        grid=(n_rows // tm,),
        in_specs=[
            _row_spec(tm, D_MODEL),
            _row_spec(tm, D_MODEL),
            _resident((N_GROUPS, CHUNK, CHUNK)),
            _resident((CHUNK, D_MODEL)),
            _resident((D_MODEL, D_MODEL)),
            _resident((1, D_MODEL)),
            _row_spec(tm, D_MODEL),
            _mod_spec(tm),
            _resident((1, D_MODEL)),
        ],
        out_specs=_row_spec(tm, D_MODEL),
        out_shape=jax.ShapeDtypeStruct((n_rows, D_MODEL), F32),
        scratch_shapes=[pltpu.VMEM((tm, D_MODEL), BF16)],
        compiler_params=_params(1),
        name="gmlp_spatial_out",
    )(u, v, w_s, b_s_wide, w, b, x, mods, g_post)


def _row(v):
    return v.reshape(1, -1)


def kernel(x, c, ctx, c_ctx, ada_w, ada_b, norm_mix_pre, norm_mix_post, norm_ffn_pre, norm_ffn_post, mla_w_dq, mla_q_norm, mla_w_uq, mla_w_dkv, mla_kv_norm, mla_w_ukv, mla_w_o, conv_w_pw1, conv_b_pw1, conv_w_dw, conv_b_dw, conv_ln_g, conv_ln_b, conv_w_pw2, conv_b_pw2, gmlp_w_uv, gmlp_b_uv, gmlp_ln_g, gmlp_ln_b, gmlp_w_s, gmlp_b_s, gmlp_w_out, gmlp_b_out, ffn_w1, ffn_w2):
    stream = jnp.concatenate([x.reshape(N_LAT, D_MODEL), ctx.reshape(N_CTX, D_MODEL)], axis=0)
    cond = jnp.concatenate([c, c_ctx[None, :], jnp.zeros((8 - BATCH - 1, D_MODEL), F32)], axis=0)
    mods_all = _ada_all(cond, ada_w, ada_b)[:, :N_SETS].reshape(DEPTH, N_SETS, N_MOD, D_MODEL)
    cos_tab, sin_tab = _rope_tables()
    zero_bias = jnp.zeros((1, D_MODEL), F32)
    ffn_w1_bf16, ffn_w2_bf16 = ffn_w1.astype(BF16), ffn_w2.astype(BF16)

    for i in range(DEPTH):
        last = i == DEPTH - 1
        kind, j = i % N_MIXERS, i // N_MIXERS
        n_rows = N_LAT if last else N_TOK
        mods = mods_all[i]
        if kind == 0:
            w_dq, w_uq, w_dkv, w_uk_t, w_uv, w_o = _mla_weights(
                mla_w_dq[j], mla_w_uq[j], mla_w_dkv[j], mla_w_ukv[j], mla_w_o[j])
            q, kt, v = _mla_proj(stream, mods, _row(norm_mix_pre[i]), w_dq, _row(mla_q_norm[j]),
                                 w_uq, w_dkv, _row(mla_kv_norm[j]), w_uk_t, w_uv, cos_tab, sin_tab)
            o = _attention(q, kt, v, n_rows)
            if not last:
                o = _attention_ctx(q, kt, v, o)
            stream = _mix_out(o, w_o, zero_bias, stream, mods, _row(norm_mix_post[i]), n_rows)
        elif kind == 1:
            g = _conv_in(stream, mods, _row(norm_mix_pre[i]), conv_w_pw1[j].astype(BF16),
                         _row(conv_b_pw1[j]))
            stream = _conv_out(g, conv_w_dw[j], _row(conv_b_dw[j]), _row(conv_ln_g[j]),
                               _row(conv_ln_b[j]), conv_w_pw2[j].astype(BF16),
                               _row(conv_b_pw2[j]), stream, mods, _row(norm_mix_post[i]), n_rows)
        else:
            u, v = _gmlp_in(stream, mods, _row(norm_mix_pre[i]), gmlp_w_uv[j].astype(BF16),
                            _row(gmlp_b_uv[j]), _row(gmlp_ln_g[j]), _row(gmlp_ln_b[j]))
            b_s_wide = jnp.repeat(gmlp_b_s[j].T, GROUP_DIM, axis=1)
            stream = _gmlp_out(u, v, gmlp_w_s[j].astype(BF16), b_s_wide,
                               gmlp_w_out[j].astype(BF16), _row(gmlp_b_out[j]), stream, mods,
                               _row(norm_mix_post[i]), n_rows)
        stream = _ffn(stream, mods, _row(norm_ffn_pre[i]), _row(norm_ffn_post[i]),
                      ffn_w1_bf16, ffn_w2_bf16, i, n_rows)
    return stream.reshape(BATCH, SEQ, D_MODEL)
```

```python
import functools

import jax
import jax.numpy as jnp
import numpy as np
from jax import lax
from jax.experimental import pallas as pl
from jax.experimental.pallas import tpu as pltpu

D_MODEL = 2048
BATCH = 2
SEQ = 4096
DEPTH = 4
CTX_LEN = 256
GRID_W = 64
N_MIXERS = 3
N_HEADS = 16
QK_NOPE_DIM = 128
QK_ROPE_DIM = 64
QK_DIM = QK_NOPE_DIM + QK_ROPE_DIM
V_HEAD_DIM = 128
Q_LORA_RANK = 768
KV_LORA_RANK = 512
ROPE_THETA = 10000.0
ATTN_SCALE = QK_DIM ** -0.5
CONV_WIDTH = 31
CONV_PAD = CONV_WIDTH // 2
CHUNK = 128
N_GROUPS = 16
GROUP_DIM = D_MODEL // N_GROUPS
D_FF = 4 * D_MODEL
N_MOD = 6
EPS = 1e-6

N_LAT = BATCH * SEQ
N_CTX = BATCH * CTX_LEN
N_TOK = N_LAT + N_CTX
N_SETS = 3
HEAD_PAD = 256
LANES = 128
SUBLANES = 8
HALO = 16
CONV_ROWS = 128
CONV_FIRST = HALO - CONV_PAD
CONV_SPAN = CONV_ROWS + SUBLANES * ((CONV_FIRST + CONV_WIDTH - 1) // SUBLANES)
VMEM_LIMIT = 56 * 1024 * 1024
ROW_CHUNK = 16
ROW_CHUNK_UNROLL = 4

F32 = jnp.float32
BF16 = jnp.bfloat16


def _dot(a, b):
    return jnp.dot(a, b, preferred_element_type=F32)


def _rms(x, g):
    return x * lax.rsqrt(jnp.mean(x * x, axis=-1, keepdims=True) + EPS) * g


def _layer_norm(x, g, b):
    mu = jnp.mean(x, axis=-1, keepdims=True)
    xc = x - mu
    var = jnp.mean(xc * xc, axis=-1, keepdims=True)
    return xc * lax.rsqrt(var + EPS) * g + b


def _sigmoid(x):
    return 1.0 / (1.0 + jnp.exp(-x))


def _gelu(x):
    return 0.5 * x * (1.0 + lax.erf(x * np.float32(np.sqrt(0.5))))


def _premix(x, mod, g, shift_idx):
    return _rms(x, g) * (1.0 + mod[shift_idx + 1:shift_idx + 2]) + mod[shift_idx:shift_idx + 1]


def _for_row_chunks(n_rows, body):
    def step(r, carry):
        body(pl.ds(pl.multiple_of(r * ROW_CHUNK, ROW_CHUNK), ROW_CHUNK))
        return carry

    lax.fori_loop(0, n_rows // ROW_CHUNK, step, 0, unroll=ROW_CHUNK_UNROLL)


def _params(n_grid_axes):
    return pltpu.CompilerParams(
        dimension_semantics=("arbitrary",) * n_grid_axes,
        vmem_limit_bytes=VMEM_LIMIT)


def _resident(shape):
    nd = len(shape)
    return pl.BlockSpec(shape, lambda *_: (0,) * nd, pipeline_mode=pl.Buffered(1))


def _mod_spec(tm):
    return pl.BlockSpec((1, N_MOD, D_MODEL),
                        lambda i, *_: (jnp.minimum(i * tm // SEQ, N_SETS - 1), 0, 0))


def _row_spec(tm, width):
    return pl.BlockSpec((tm, width), lambda i, *_: (i, 0))


def _stream_operands(stream, tm):
    if not isinstance(stream, tuple):
        return (stream,), (_row_spec(tm, D_MODEL),)
    lat_tiles = N_LAT // tm
    return stream, (
        pl.BlockSpec((tm, D_MODEL), lambda i, *_: (jnp.minimum(i, lat_tiles - 1), 0)),
        pl.BlockSpec((tm, D_MODEL), lambda i, *_: (jnp.maximum(i - lat_tiles, 0), 0)))


def _stream_tile(x_refs):
    if len(x_refs) == 1:
        return x_refs[0][...]
    lat_ref, ctx_ref = x_refs
    return jnp.where(pl.program_id(0) < N_LAT // lat_ref.shape[0], lat_ref[...], ctx_ref[...])


def _ada_kernel(cond_ref, w_ref, b_ref, o_ref):
    c = cond_ref[...]
    s = (c * _sigmoid(c)).astype(BF16)
    o_ref[0] = _dot(s, w_ref[0].astype(BF16)) + b_ref[0]


def _ada_all(cond, ada_w, ada_b):
    tn = 1024
    n_out = N_MOD * D_MODEL
    return pl.pallas_call(
        _ada_kernel,
        grid=(DEPTH, n_out // tn),
        in_specs=[
            pl.BlockSpec((8, D_MODEL), lambda l, j: (0, 0)),
            pl.BlockSpec((1, D_MODEL, tn), lambda l, j: (l, 0, j)),
            pl.BlockSpec((1, 1, tn), lambda l, j: (l, 0, j)),
        ],
        out_specs=pl.BlockSpec((1, 8, tn), lambda l, j: (l, 0, j)),
        out_shape=jax.ShapeDtypeStruct((DEPTH, 8, n_out), F32),
        compiler_params=_params(2),
        name="ada_mod",
    )(cond, ada_w, ada_b.reshape(DEPTH, 1, n_out))


def _post_residual(y, x, mod, g_post, gate_idx):
    return x + mod[gate_idx:gate_idx + 1] * _rms(y, g_post)


def _ffn_kernel(n_f, x_ref, mod_ref, gpre_ref, gpost_ref, w1_ref, w2_ref, o_ref, h_scr, acc_scr):
    f = pl.program_id(1)

    n_rows = h_scr.shape[0]

    @pl.when(f == 0)
    def _():
        mod = mod_ref[0]
        gain = gpre_ref[...] * (1.0 + mod[4:5])
        shift = mod[3:4]

        def chunk(rows):
            x = x_ref[rows, :]
            inv = lax.rsqrt(jnp.mean(x * x, axis=-1, keepdims=True) + EPS)
            h_scr[rows, :] = (x * inv * gain + shift).astype(BF16)

        _for_row_chunks(n_rows, chunk)
        acc_scr[...] = jnp.zeros(acc_scr.shape, F32)

    a = _dot(h_scr[...], w1_ref[0])
    a = jnp.square(jnp.maximum(a, 0.0)).astype(BF16)
    acc_scr[...] += _dot(a, w2_ref[0])

    @pl.when(f == n_f - 1)
    def _():
        gain = mod_ref[0][5:6] * gpost_ref[...]

        def chunk(rows):
            y = acc_scr[rows, :]
            inv = lax.rsqrt(jnp.mean(y * y, axis=-1, keepdims=True) + EPS)
            o_ref[rows, :] = x_ref[rows, :] + y * inv * gain

        _for_row_chunks(n_rows, chunk)


def _ffn(x, mods, g_pre, g_post, w1, w2, layer, n_rows):
    tm, tf = 512, 1024
    n_f = D_FF // tf
    return pl.pallas_call(
        functools.partial(_ffn_kernel, n_f),
        grid=(n_rows // tm, n_f),
        in_specs=[
            _row_spec(tm, D_MODEL),
            _mod_spec(tm),
            pl.BlockSpec((1, D_MODEL), lambda i, f: (0, 0)),
            pl.BlockSpec((1, D_MODEL), lambda i, f: (0, 0)),
            pl.BlockSpec((1, D_MODEL, tf), lambda i, f: (layer, 0, f)),
            pl.BlockSpec((1, tf, D_MODEL), lambda i, f: (layer, f, 0)),
        ],
        out_specs=_row_spec(tm, D_MODEL),
        out_shape=jax.ShapeDtypeStruct((n_rows, D_MODEL), F32),
        scratch_shapes=[pltpu.VMEM((tm, D_MODEL), BF16), pltpu.VMEM((tm, D_MODEL), F32)],
        compiler_params=_params(2),
        name="ffn",
    )(x, mods, g_pre, g_post, w1, w2)


def _mix_out_kernel(n_x, a_ref, w_ref, b_ref, *refs):
    x_refs, (mod_ref, gpost_ref, o_ref) = refs[:n_x], refs[n_x:]
    y = _dot(a_ref[...], w_ref[...]) + b_ref[...]
    o_ref[...] = _post_residual(y, _stream_tile(x_refs), mod_ref[0], gpost_ref[...], 2)


def _mix_out(a, w, b, stream, mods, g_post, n_rows):
    tm = 256
    xs, x_specs = _stream_operands(stream, tm)
    return pl.pallas_call(
        functools.partial(_mix_out_kernel, len(xs)),
        grid=(n_rows // tm,),
        in_specs=[
            _row_spec(tm, D_MODEL),
            _resident((D_MODEL, D_MODEL)),
            _resident((1, D_MODEL)),
            *x_specs,
            _mod_spec(tm),
            _resident((1, D_MODEL)),
        ],
        out_specs=_row_spec(tm, D_MODEL),
        out_shape=jax.ShapeDtypeStruct((n_rows, D_MODEL), F32),
        compiler_params=_params(1),
        name="mix_out",
    )(a, w, b, *xs, mods, g_post)


def _rope_tables():
    f32 = np.float32
    t = np.arange(SEQ, dtype=np.int32)
    row = (t // GRID_W).astype(f32)
    col = (t % GRID_W).astype(f32)
    half = QK_ROPE_DIM // 2
    inv = (f32(ROPE_THETA) ** (-np.arange(0, half, 2, dtype=f32) / f32(half))).astype(f32)
    ang_r, ang_c = row[:, None] * inv, col[:, None] * inv
    cr, sr, cc, sc = np.cos(ang_r), np.sin(ang_r), np.cos(ang_c), np.sin(ang_c)
    zeros = np.zeros((SEQ, QK_ROPE_DIM), f32)
    cos_tab = np.concatenate([cr, cr, cc, cc, zeros], axis=1)
    sin_tab = np.concatenate([-sr, sr, -sc, sc, zeros], axis=1)
    ident = np.concatenate([np.ones((CTX_LEN, QK_ROPE_DIM), f32),
                            np.zeros((CTX_LEN, QK_ROPE_DIM), f32)], axis=1)
    cos_tab = np.concatenate([cos_tab, ident], axis=0)
    sin_tab = np.concatenate([sin_tab, np.zeros((CTX_LEN, LANES), f32)], axis=0)
    return jnp.asarray(cos_tab, F32), jnp.asarray(sin_tab, F32)


def _swap_half_lanes(w):
    perm = np.arange(QK_ROPE_DIM) ^ (QK_ROPE_DIM // 4)
    return w[..., perm]


def _mla_weights(w_dq, w_uq, w_dkv, w_ukv, w_o):
    uq = w_uq.reshape(Q_LORA_RANK, N_HEADS, QK_DIM)
    rope = uq[..., QK_NOPE_DIM:]
    uq = jnp.concatenate([uq[..., :QK_NOPE_DIM], rope, _swap_half_lanes(rope)], axis=-1)
    kpe = w_dkv[:, KV_LORA_RANK:]
    dkv = jnp.concatenate([w_dkv[:, :KV_LORA_RANK], kpe, _swap_half_lanes(kpe)], axis=-1)
    ukv = w_ukv.reshape(KV_LORA_RANK, N_HEADS, QK_NOPE_DIM + V_HEAD_DIM)
    uk_t = ukv[..., :QK_NOPE_DIM].reshape(KV_LORA_RANK, N_HEADS * QK_NOPE_DIM).T
    uv = ukv[..., QK_NOPE_DIM:].reshape(KV_LORA_RANK, N_HEADS * V_HEAD_DIM)
    return (w_dq.astype(BF16), uq.reshape(Q_LORA_RANK, N_HEADS * HEAD_PAD).astype(BF16),
            dkv.astype(BF16), uk_t.astype(BF16), uv.astype(BF16), w_o.astype(BF16))


def _rope(blk, cos_tab, sin_tab):
    return blk * cos_tab + pltpu.roll(blk, LANES // 2, axis=1) * sin_tab


def _mla_proj_kernel(n_x, *refs):
    x_refs = refs[:n_x]
    (mod_ref, gpre_ref, wdq_ref, qn_ref, wuq_ref, wdkv_ref, kvn_ref, wukt_ref, wuv_ref, cos_ref,
     sin_ref, q_ref, kt_ref, v_ref) = refs[n_x:]
    h = _premix(_stream_tile(x_refs), mod_ref[0], gpre_ref[...], 0).astype(BF16)
    cos_tab, sin_tab = cos_ref[...], sin_ref[...]
    q_lat = _rms(_dot(h, wdq_ref[...]), qn_ref[...]).astype(BF16)
    kv_a = _dot(h, wdkv_ref[...])
    c_kv = _rms(kv_a[:, :KV_LORA_RANK], kvn_ref[...]).astype(BF16)
    k_pe_t = _rope(kv_a[:, KV_LORA_RANK:], cos_tab, sin_tab).T.astype(BF16)
    k_nope_t = _qk(wukt_ref[...], c_kv)
    v_ref[...] = _dot(c_kv, wuv_ref[...]).astype(BF16)
    for hd in range(N_HEADS):
        lo = hd * HEAD_PAD
        q = _dot(q_lat, wuq_ref[:, lo:lo + HEAD_PAD])
        q_ref[:, lo:lo + LANES] = (q[:, :LANES] * ATTN_SCALE).astype(BF16)
        q_ref[:, lo + LANES:lo + HEAD_PAD] = (
            _rope(q[:, LANES:], cos_tab, sin_tab) * ATTN_SCALE).astype(BF16)
        kt_ref[lo:lo + LANES, :] = k_nope_t[hd * LANES:(hd + 1) * LANES].astype(BF16)
        kt_ref[lo + LANES:lo + HEAD_PAD, :] = k_pe_t


def _mla_proj(stream, mods, g_pre, w_dq, q_norm, w_uq, w_dkv, kv_norm, w_uk_t, w_uv, cos_tab,
              sin_tab):
    tm = 256
    xs, x_specs = _stream_operands(stream, tm)
    lat_tiles, seq_tiles = N_LAT // tm, SEQ // tm
    tab_spec = pl.BlockSpec(
        (tm, LANES), lambda i: (jnp.where(i < lat_tiles, i % seq_tiles, seq_tiles), 0))
    wide = N_HEADS * HEAD_PAD
    return pl.pallas_call(
        functools.partial(_mla_proj_kernel, len(xs)),
        grid=(N_TOK // tm,),
        in_specs=[
            *x_specs,
            _mod_spec(tm),
            _resident((1, D_MODEL)),
            _resident((D_MODEL, Q_LORA_RANK)),
            _resident((1, Q_LORA_RANK)),
            _resident((Q_LORA_RANK, wide)),
            _resident((D_MODEL, KV_LORA_RANK + LANES)),
            _resident((1, KV_LORA_RANK)),
            _resident((N_HEADS * QK_NOPE_DIM, KV_LORA_RANK)),
            _resident((KV_LORA_RANK, N_HEADS * V_HEAD_DIM)),
            tab_spec,
            tab_spec,
        ],
        out_specs=[_row_spec(tm, wide), pl.BlockSpec((wide, tm), lambda i: (0, i)),
                   _row_spec(tm, D_MODEL)],
        out_shape=[jax.ShapeDtypeStruct((N_TOK, wide), BF16),
                   jax.ShapeDtypeStruct((wide, N_TOK), BF16),
                   jax.ShapeDtypeStruct((N_TOK, D_MODEL), BF16)],
        compiler_params=_params(1),
        name="mla_proj",
    )(*xs, mods, g_pre, w_dq, q_norm, w_uq, w_dkv, kv_norm, w_uk_t, w_uv, cos_tab, sin_tab)


def _qk(q, k):
    return lax.dot_general(q, k, (((1,), (1,)), ((), ())), preferred_element_type=F32)


def _attn_kernel(tq, tk, q_ref, ktc_ref, ktl_ref, vc_ref, vl_ref, qn_ref, ktcn_ref, ktln_ref, o_ref,
                 s_even, s_odd, m_even, m_odd, vaug_scr):
    vaug_scr[0:CTX_LEN, 0:V_HEAD_DIM] = vc_ref[...]
    vaug_scr[CTX_LEN:, 0:V_HEAD_DIM] = vl_ref[...]
    vaug_scr[:, V_HEAD_DIM:] = jnp.ones((CTX_LEN + SEQ, LANES), BF16)

    chunks = [(0, CTX_LEN)] + [(CTX_LEN + c * tk, tk) for c in range(SEQ // tk)]
    n_tiles = SEQ // tq

    def rows(tile):
        start = tile * tq if isinstance(tile, int) else pl.multiple_of(tile * tq, tq)
        return pl.ds(start, tq)

    def scores(q, kc_ref, kl_ref, s_scr, m_scr):
        for n, (off, width) in enumerate(chunks):
            kt = kc_ref[...] if n == 0 else kl_ref[:, off - CTX_LEN:off - CTX_LEN + width]
            s = _dot(q, kt)
            s_scr[:, off:off + width] = s
            cm = s[:, 0:LANES]
            for t in range(1, width // LANES):
                cm = jnp.maximum(cm, s[:, t * LANES:(t + 1) * LANES])
            m_scr[...] = cm if n == 0 else jnp.maximum(m_scr[...], cm)

    def outputs(tile, s_scr, m_scr):
        m = jnp.max(m_scr[...], axis=1, keepdims=True)
        acc = None
        for off, width in chunks:
            p = jnp.exp(s_scr[:, off:off + width] - m).astype(BF16)
            part = _dot(p, vaug_scr[off:off + width, :])
            acc = part if acc is None else acc + part
        o_ref[rows(tile), :] = (acc[:, :V_HEAD_DIM] / acc[:, V_HEAD_DIM:]).astype(BF16)

    def own_scores(tile, s_scr, m_scr):
        scores(q_ref[rows(tile), :], ktc_ref, ktl_ref, s_scr, m_scr)

    @pl.when(pl.program_id(0) == 0)
    def _():
        own_scores(0, s_even, m_even)

    def tile_pair(j, carry):
        t = 2 * j
        own_scores(t + 1, s_odd, m_odd)
        outputs(t, s_even, m_even)
        own_scores(t + 2, s_even, m_even)
        outputs(t + 1, s_odd, m_odd)
        return carry

    lax.fori_loop(0, n_tiles // 2 - 1, tile_pair, 0)
    own_scores(n_tiles - 1, s_odd, m_odd)
    outputs(n_tiles - 2, s_even, m_even)
    scores(qn_ref[...], ktcn_ref, ktln_ref, s_even, m_even)
    outputs(n_tiles - 1, s_odd, m_odd)


def _attention(q, kt, v, n_out_rows):
    tq, tk = 512, 256
    ctx_blk = N_LAT // CTX_LEN
    n_keys = CTX_LEN + SEQ
    n_steps = BATCH * N_HEADS

    def this(s):
        return s // N_HEADS, s % N_HEADS

    def following(s):
        return this(jnp.minimum(s + 1, n_steps - 1))

    def spec(shape, pick, index):
        return pl.BlockSpec(shape, lambda s: index(*pick(s)))

    return pl.pallas_call(
        functools.partial(_attn_kernel, tq, tk),
        grid=(n_steps,),
        in_specs=[
            spec((SEQ, HEAD_PAD), this, lambda b, h: (b, h)),
            spec((HEAD_PAD, CTX_LEN), this, lambda b, h: (h, ctx_blk + b)),
            spec((HEAD_PAD, SEQ), this, lambda b, h: (h, b)),
            spec((CTX_LEN, V_HEAD_DIM), this, lambda b, h: (ctx_blk + b, h)),
            spec((SEQ, V_HEAD_DIM), this, lambda b, h: (b, h)),
            spec((tq, HEAD_PAD), following, lambda b, h: (b * (SEQ // tq), h)),
            spec((HEAD_PAD, CTX_LEN), following, lambda b, h: (h, ctx_blk + b)),
            spec((HEAD_PAD, SEQ), following, lambda b, h: (h, b)),
        ],
        out_specs=spec((SEQ, V_HEAD_DIM), this, lambda b, h: (b, h)),
        out_shape=jax.ShapeDtypeStruct((n_out_rows, N_HEADS * V_HEAD_DIM), BF16),
        scratch_shapes=[pltpu.VMEM((tq, n_keys), F32), pltpu.VMEM((tq, n_keys), F32),
                        pltpu.VMEM((tq, LANES), F32), pltpu.VMEM((tq, LANES), F32),
                        pltpu.VMEM((n_keys, V_HEAD_DIM + LANES), BF16)],
        compiler_params=_params(1),
        name="mla_attention",
    )(q, kt, kt, v, v, q, kt, kt)


def _attn_ctx_kernel(q_ref, kt_ref, v_ref, o_in_ref, o_ref):
    del o_in_ref
    s = _dot(q_ref[...], kt_ref[...])
    p = jnp.exp(s - jnp.max(s, axis=1, keepdims=True))
    l = jnp.sum(p, axis=1, keepdims=True)
    o_ref[...] = (_dot(p.astype(BF16), v_ref[...]) / l).astype(BF16)


def _attention_ctx(q, kt, v, o):
    ctx_blk = N_LAT // CTX_LEN
    q_spec = pl.BlockSpec((CTX_LEN, HEAD_PAD), lambda b, h: (ctx_blk + b, h))
    kt_spec = pl.BlockSpec((HEAD_PAD, CTX_LEN), lambda b, h: (h, ctx_blk + b))
    v_spec = pl.BlockSpec((CTX_LEN, V_HEAD_DIM), lambda b, h: (ctx_blk + b, h))
    return pl.pallas_call(
        _attn_ctx_kernel,
        grid=(BATCH, N_HEADS),
        in_specs=[q_spec, kt_spec, v_spec, pl.BlockSpec(memory_space=pl.ANY)],
        out_specs=v_spec,
        out_shape=jax.ShapeDtypeStruct(o.shape, o.dtype),
        input_output_aliases={3: 0},
        compiler_params=_params(2),
        name="mla_attention_ctx",
    )(q, kt, v, o)


def _conv_in_kernel(x_ref, mod_ref, gpre_ref, w_ref, b_ref, o_ref):
    h = _premix(x_ref[...], mod_ref[0], gpre_ref[...], 0).astype(BF16)
    a = _dot(h, w_ref[:, :D_MODEL]) + b_ref[:, :D_MODEL]
    gate = _dot(h, w_ref[:, D_MODEL:]) + b_ref[:, D_MODEL:]
    o_ref[...] = a * _sigmoid(gate)


def _conv_in(x, mods, g_pre, w, b):
    tm = 256
    return pl.pallas_call(
        _conv_in_kernel,
        grid=(N_TOK // tm,),
        in_specs=[
            _row_spec(tm, D_MODEL),
            _mod_spec(tm),
            _resident((1, D_MODEL)),
            _resident((D_MODEL, 2 * D_MODEL)),
            _resident((1, 2 * D_MODEL)),
        ],
        out_specs=_row_spec(tm, D_MODEL),
        out_shape=jax.ShapeDtypeStruct((N_TOK, D_MODEL), F32),
        compiler_params=_params(1),
        name="conv_in_glu",
    )(x, mods, g_pre, w, b)


def _conv_out_kernel(tm, g_ref, left_ref, right_ref, wdw_ref, bdw_ref, lng_ref, lnb_ref, w2_ref,
                     b2_ref, x_ref, mod_ref, gpost_ref, o_ref, pad_scr, y_scr, shift_scr):
    i = pl.program_id(0)
    seg_tiles = SEQ // tm
    in_lat = i < N_LAT // tm
    has_left = jnp.logical_and(in_lat, i % seg_tiles != 0)
    has_right = jnp.logical_and(in_lat, i % seg_tiles != seg_tiles - 1)
    pad_scr[0:HALO, :] = jnp.where(has_left, left_ref[...], 0.0)
    pad_scr[HALO:HALO + tm, :] = g_ref[...]
    pad_scr[HALO + tm:HALO + tm + HALO, :] = jnp.where(has_right, right_ref[...], 0.0)

    rows, cols, sub, first, span = CONV_ROWS, LANES, SUBLANES, CONV_FIRST, CONV_SPAN
    for r0 in range(0, tm, rows):
        for c0 in range(0, D_MODEL, cols):
            acc = bdw_ref[:, c0:c0 + cols]
            for res in range(sub):
                if res:
                    shift_scr[res] = pad_scr[r0 + res:r0 + res + span, c0:c0 + cols]
                    shifted = shift_scr[res]
                else:
                    shifted = pad_scr[r0:r0 + span, c0:c0 + cols]
                for off in range(res, first + CONV_WIDTH, sub):
                    if off >= first:
                        lo = off - res
                        acc = acc + (wdw_ref[off - first:off - first + 1, c0:c0 + cols]
                                     * shifted[lo:lo + rows])
            y_scr[r0:r0 + rows, c0:c0 + cols] = acc

    y = _layer_norm(y_scr[...], lng_ref[...], lnb_ref[...])
    y = (y * _sigmoid(y)).astype(BF16)
    out = _dot(y, w2_ref[...]) + b2_ref[...]
    o_ref[...] = _post_residual(out, x_ref[...], mod_ref[0], gpost_ref[...], 2)


def _conv_out(g, w_dw, b_dw, ln_g, ln_b, w2, b2, x, mods, g_post, n_rows):
    tm = 256
    halo_per_tile = tm // HALO
    last_halo = N_TOK // HALO - 1
    return pl.pallas_call(
        functools.partial(_conv_out_kernel, tm),
        grid=(n_rows // tm,),
        in_specs=[
            _row_spec(tm, D_MODEL),
            pl.BlockSpec((HALO, D_MODEL), lambda i: (jnp.maximum(i * halo_per_tile - 1, 0), 0)),
            pl.BlockSpec((HALO, D_MODEL),
                         lambda i: (jnp.minimum((i + 1) * halo_per_tile, last_halo), 0)),
            _resident((CONV_WIDTH, D_MODEL)),
            _resident((1, D_MODEL)),
            _resident((1, D_MODEL)),
            _resident((1, D_MODEL)),
            _resident((D_MODEL, D_MODEL)),
            _resident((1, D_MODEL)),
            _row_spec(tm, D_MODEL),
            _mod_spec(tm),
            _resident((1, D_MODEL)),
        ],
        out_specs=_row_spec(tm, D_MODEL),
        out_shape=jax.ShapeDtypeStruct((n_rows, D_MODEL), F32),
        scratch_shapes=[pltpu.VMEM((tm + 2 * HALO, D_MODEL), F32), pltpu.VMEM((tm, D_MODEL), F32),
                        pltpu.VMEM((SUBLANES, CONV_SPAN, LANES), F32)],
        compiler_params=_params(1),
        name="conv_dw_out",
    )(g, g, g, w_dw, b_dw, ln_g, ln_b, w2, b2, x, mods, g_post)


def _gmlp_in_kernel(x_ref, mod_ref, gpre_ref, w_ref, b_ref, lng_ref, lnb_ref, u_ref, v_ref):
    h = _premix(x_ref[...], mod_ref[0], gpre_ref[...], 0).astype(BF16)
    u_ref[...] = _gelu(_dot(h, w_ref[:, :D_MODEL]) + b_ref[:, :D_MODEL])
    v = _gelu(_dot(h, w_ref[:, D_MODEL:]) + b_ref[:, D_MODEL:])
    v_ref[...] = _layer_norm(v, lng_ref[...], lnb_ref[...]).astype(BF16)


def _gmlp_in(x, mods, g_pre, w_uv, b_uv, ln_g, ln_b):
    tm = 256
    return pl.pallas_call(
        _gmlp_in_kernel,
        grid=(N_TOK // tm,),
        in_specs=[
            _row_spec(tm, D_MODEL),
            _mod_spec(tm),
            _resident((1, D_MODEL)),
            _resident((D_MODEL, 2 * D_MODEL)),
            _resident((1, 2 * D_MODEL)),
            _resident((1, D_MODEL)),
            _resident((1, D_MODEL)),
        ],
        out_specs=[_row_spec(tm, D_MODEL), _row_spec(tm, D_MODEL)],
        out_shape=[jax.ShapeDtypeStruct((N_TOK, D_MODEL), F32),
                   jax.ShapeDtypeStruct((N_TOK, D_MODEL), BF16)],
        compiler_params=_params(1),
        name="gmlp_in",
    )(x, mods, g_pre, w_uv, b_uv, ln_g, ln_b)


def _gmlp_out_kernel(tm, u_ref, v_ref, ws_ref, bs_ref, w_ref, b_ref, x_ref, mod_ref, gpost_ref,
                     o_ref, t_scr):
    for g in range(N_GROUPS):
        c0 = g * GROUP_DIM
        w_s = ws_ref[g]
        for r0 in range(0, tm, CHUNK):
            sv = _dot(w_s, v_ref[r0:r0 + CHUNK, c0:c0 + GROUP_DIM]) + bs_ref[:, c0:c0 + GROUP_DIM]
            t_scr[r0:r0 + CHUNK, c0:c0 + GROUP_DIM] = (
                u_ref[r0:r0 + CHUNK, c0:c0 + GROUP_DIM] * sv).astype(BF16)
    out = _dot(t_scr[...], w_ref[...]) + b_ref[...]
    o_ref[...] = _post_residual(out, x_ref[...], mod_ref[0], gpost_ref[...], 2)


def _gmlp_out(u, v, w_s, b_s_wide, w, b, x, mods, g_post, n_rows):
    tm = 256
    return pl.pallas_call(
        functools.partial(_gmlp_out_kernel, tm),
        grid=(n_rows // tm,),
        in_specs=[
            _row_spec(tm, D_MODEL),
            _row_spec(tm, D_MODEL),
            _resident((N_GROUPS, CHUNK, CHUNK)),
            _resident((CHUNK, D_MODEL)),
            _resident((D_MODEL, D_MODEL)),
            _resident((1, D_MODEL)),
            _row_spec(tm, D_MODEL),
            _mod_spec(tm),
            _resident((1, D_MODEL)),
        ],
        out_specs=_row_spec(tm, D_MODEL),
        out_shape=jax.ShapeDtypeStruct((n_rows, D_MODEL), F32),
        scratch_shapes=[pltpu.VMEM((tm, D_MODEL), BF16)],
        compiler_params=_params(1),
        name="gmlp_spatial_out",
    )(u, v, w_s, b_s_wide, w, b, x, mods, g_post)


def _row(v):
    return v.reshape(1, -1)


def kernel(x, c, ctx, c_ctx, ada_w, ada_b, norm_mix_pre, norm_mix_post, norm_ffn_pre, norm_ffn_post, mla_w_dq, mla_q_norm, mla_w_uq, mla_w_dkv, mla_kv_norm, mla_w_ukv, mla_w_o, conv_w_pw1, conv_b_pw1, conv_w_dw, conv_b_dw, conv_ln_g, conv_ln_b, conv_w_pw2, conv_b_pw2, gmlp_w_uv, gmlp_b_uv, gmlp_ln_g, gmlp_ln_b, gmlp_w_s, gmlp_b_s, gmlp_w_out, gmlp_b_out, ffn_w1, ffn_w2):
    stream = (x.reshape(N_LAT, D_MODEL), ctx.reshape(N_CTX, D_MODEL))
    cond = jnp.concatenate([c, c_ctx[None, :], jnp.zeros((8 - BATCH - 1, D_MODEL), F32)], axis=0)
    mods_all = _ada_all(cond, ada_w, ada_b)[:, :N_SETS].reshape(DEPTH, N_SETS, N_MOD, D_MODEL)
    cos_tab, sin_tab = _rope_tables()
    zero_bias = jnp.zeros((1, D_MODEL), F32)
    ffn_w1_bf16, ffn_w2_bf16 = ffn_w1.astype(BF16), ffn_w2.astype(BF16)

    for i in range(DEPTH):
        last = i == DEPTH - 1
        kind, j = i % N_MIXERS, i // N_MIXERS
        n_rows = N_LAT if last else N_TOK
        mods = mods_all[i]
        if kind == 0:
            w_dq, w_uq, w_dkv, w_uk_t, w_uv, w_o = _mla_weights(
                mla_w_dq[j], mla_w_uq[j], mla_w_dkv[j], mla_w_ukv[j], mla_w_o[j])
            q, kt, v = _mla_proj(stream, mods, _row(norm_mix_pre[i]), w_dq, _row(mla_q_norm[j]),
                                 w_uq, w_dkv, _row(mla_kv_norm[j]), w_uk_t, w_uv, cos_tab, sin_tab)
            o = _attention(q, kt, v, n_rows)
            if not last:
                o = _attention_ctx(q, kt, v, o)
            stream = _mix_out(o, w_o, zero_bias, stream, mods, _row(norm_mix_post[i]), n_rows)
        elif kind == 1:
            g = _conv_in(stream, mods, _row(norm_mix_pre[i]), conv_w_pw1[j].astype(BF16),
                         _row(conv_b_pw1[j]))
            stream = _conv_out(g, conv_w_dw[j], _row(conv_b_dw[j]), _row(conv_ln_g[j]),
                               _row(conv_ln_b[j]), conv_w_pw2[j].astype(BF16),
                               _row(conv_b_pw2[j]), stream, mods, _row(norm_mix_post[i]), n_rows)
        else:
            u, v = _gmlp_in(stream, mods, _row(norm_mix_pre[i]), gmlp_w_uv[j].astype(BF16),
                            _row(gmlp_b_uv[j]), _row(gmlp_ln_g[j]), _row(gmlp_ln_b[j]))
            b_s_wide = jnp.repeat(gmlp_b_s[j].T, GROUP_DIM, axis=1)
            stream = _gmlp_out(u, v, gmlp_w_s[j].astype(BF16), b_s_wide,
                               gmlp_w_out[j].astype(BF16), _row(gmlp_b_out[j]), stream, mods,
                               _row(norm_mix_post[i]), n_rows)
        stream = _ffn(stream, mods, _row(norm_ffn_pre[i]), _row(norm_ffn_post[i]),
                      ffn_w1_bf16, ffn_w2_bf16, i, n_rows)
    return stream.reshape(BATCH, SEQ, D_MODEL)
```

```python
import functools

import jax
import jax.numpy as jnp
import numpy as np
from jax import lax
from jax.experimental import pallas as pl
from jax.experimental.pallas import tpu as pltpu

D_MODEL = 2048
BATCH = 2
SEQ = 4096
DEPTH = 4
CTX_LEN = 256
GRID_W = 64
N_MIXERS = 3
N_HEADS = 16
QK_NOPE_DIM = 128
QK_ROPE_DIM = 64
QK_DIM = QK_NOPE_DIM + QK_ROPE_DIM
V_HEAD_DIM = 128
Q_LORA_RANK = 768
KV_LORA_RANK = 512
ROPE_THETA = 10000.0
ATTN_SCALE = QK_DIM ** -0.5
CONV_WIDTH = 31
CONV_PAD = CONV_WIDTH // 2
CHUNK = 128
N_GROUPS = 16
GROUP_DIM = D_MODEL // N_GROUPS
D_FF = 4 * D_MODEL
N_MOD = 6
EPS = 1e-6

N_LAT = BATCH * SEQ
N_CTX = BATCH * CTX_LEN
N_TOK = N_LAT + N_CTX
N_SETS = 3
HEAD_PAD = 256
LANES = 128
SUBLANES = 8
HALO = 16
CONV_ROWS = 128
CONV_FIRST = HALO - CONV_PAD
CONV_SPAN = CONV_ROWS + SUBLANES * ((CONV_FIRST + CONV_WIDTH - 1) // SUBLANES)
VMEM_LIMIT = 56 * 1024 * 1024
ROW_CHUNK = 16
ROW_CHUNK_UNROLL = 4

F32 = jnp.float32
BF16 = jnp.bfloat16


def _dot(a, b):
    return jnp.dot(a, b, preferred_element_type=F32)


def _rms(x, g):
    return x * lax.rsqrt(jnp.mean(x * x, axis=-1, keepdims=True) + EPS) * g


def _layer_norm(x, g, b):
    mu = jnp.mean(x, axis=-1, keepdims=True)
    xc = x - mu
    var = jnp.mean(xc * xc, axis=-1, keepdims=True)
    return xc * lax.rsqrt(var + EPS) * g + b


def _sigmoid(x):
    return 1.0 / (1.0 + jnp.exp(-x))


def _gelu(x):
    return 0.5 * x * (1.0 + lax.erf(x * np.float32(np.sqrt(0.5))))


def _premix(x, mod, g, shift_idx):
    return _rms(x, g) * (1.0 + mod[shift_idx + 1:shift_idx + 2]) + mod[shift_idx:shift_idx + 1]


def _for_row_chunks(n_rows, body):
    def step(r, carry):
        body(pl.ds(pl.multiple_of(r * ROW_CHUNK, ROW_CHUNK), ROW_CHUNK))
        return carry

    lax.fori_loop(0, n_rows // ROW_CHUNK, step, 0, unroll=ROW_CHUNK_UNROLL)


def _params(n_grid_axes):
    return pltpu.CompilerParams(
        dimension_semantics=("arbitrary",) * n_grid_axes,
        vmem_limit_bytes=VMEM_LIMIT)


def _resident(shape):
    nd = len(shape)
    return pl.BlockSpec(shape, lambda *_: (0,) * nd, pipeline_mode=pl.Buffered(1))


def _mod_spec(tm):
    return pl.BlockSpec((1, N_MOD, D_MODEL),
                        lambda i, *_: (jnp.minimum(i * tm // SEQ, N_SETS - 1), 0, 0))


def _row_spec(tm, width):
    return pl.BlockSpec((tm, width), lambda i, *_: (i, 0))


def _stream_operands(stream, tm):
    if not isinstance(stream, tuple):
        return (stream,), (_row_spec(tm, D_MODEL),)
    lat_tiles = N_LAT // tm
    return stream, (
        pl.BlockSpec((tm, D_MODEL), lambda i, *_: (jnp.minimum(i, lat_tiles - 1), 0)),
        pl.BlockSpec((tm, D_MODEL), lambda i, *_: (jnp.maximum(i - lat_tiles, 0), 0)))


def _stream_tile(x_refs):
    if len(x_refs) == 1:
        return x_refs[0][...]
    lat_ref, ctx_ref = x_refs
    return jnp.where(pl.program_id(0) < N_LAT // lat_ref.shape[0], lat_ref[...], ctx_ref[...])


def _ada_kernel(cond_ref, w_ref, b_ref, o_ref):
    c = cond_ref[...]
    s = (c * _sigmoid(c)).astype(BF16)
    o_ref[0] = _dot(s, w_ref[0].astype(BF16)) + b_ref[0]


def _ada_all(cond, ada_w, ada_b):
    tn = 1024
    n_out = N_MOD * D_MODEL
    return pl.pallas_call(
        _ada_kernel,
        grid=(DEPTH, n_out // tn),
        in_specs=[
            pl.BlockSpec((8, D_MODEL), lambda l, j: (0, 0)),
            pl.BlockSpec((1, D_MODEL, tn), lambda l, j: (l, 0, j)),
            pl.BlockSpec((1, 1, tn), lambda l, j: (l, 0, j)),
        ],
        out_specs=pl.BlockSpec((1, 8, tn), lambda l, j: (l, 0, j)),
        out_shape=jax.ShapeDtypeStruct((DEPTH, 8, n_out), F32),
        compiler_params=_params(2),
        name="ada_mod",
    )(cond, ada_w, ada_b.reshape(DEPTH, 1, n_out))


def _post_residual(y, x, mod, g_post, gate_idx):
    return x + mod[gate_idx:gate_idx + 1] * _rms(y, g_post)


def _ffn_kernel(n_f, x_ref, mod_ref, gpre_ref, gpost_ref, w1_ref, w2_ref, o_ref, h_scr, acc_scr):
    f = pl.program_id(1)

    n_rows = h_scr.shape[0]

    @pl.when(f == 0)
    def _():
        mod = mod_ref[0]
        gain = gpre_ref[...] * (1.0 + mod[4:5])
        shift = mod[3:4]

        def chunk(rows):
            x = x_ref[rows, :]
            inv = lax.rsqrt(jnp.mean(x * x, axis=-1, keepdims=True) + EPS)
            h_scr[rows, :] = (x * inv * gain + shift).astype(BF16)

        _for_row_chunks(n_rows, chunk)
        acc_scr[...] = jnp.zeros(acc_scr.shape, F32)

    a = _dot(h_scr[...], w1_ref[0])
    a = jnp.square(jnp.maximum(a, 0.0)).astype(BF16)
    acc_scr[...] += _dot(a, w2_ref[0])

    @pl.when(f == n_f - 1)
    def _():
        gain = mod_ref[0][5:6] * gpost_ref[...]

        def chunk(rows):
            y = acc_scr[rows, :]
            inv = lax.rsqrt(jnp.mean(y * y, axis=-1, keepdims=True) + EPS)
            o_ref[rows, :] = x_ref[rows, :] + y * inv * gain

        _for_row_chunks(n_rows, chunk)


def _ffn(x, mods, g_pre, g_post, w1, w2, layer, n_rows):
    tm, tf = 512, 1024
    n_f = D_FF // tf
    return pl.pallas_call(
        functools.partial(_ffn_kernel, n_f),
        grid=(n_rows // tm, n_f),
        in_specs=[
            _row_spec(tm, D_MODEL),
            _mod_spec(tm),
            pl.BlockSpec((1, D_MODEL), lambda i, f: (0, 0)),
            pl.BlockSpec((1, D_MODEL), lambda i, f: (0, 0)),
            pl.BlockSpec((1, D_MODEL, tf), lambda i, f: (layer, 0, f)),
            pl.BlockSpec((1, tf, D_MODEL), lambda i, f: (layer, f, 0)),
        ],
        out_specs=_row_spec(tm, D_MODEL),
        out_shape=jax.ShapeDtypeStruct((n_rows, D_MODEL), F32),
        scratch_shapes=[pltpu.VMEM((tm, D_MODEL), BF16), pltpu.VMEM((tm, D_MODEL), F32)],
        compiler_params=_params(2),
        name="ffn",
    )(x, mods, g_pre, g_post, w1, w2)


def _mix_out_kernel(n_x, a_ref, w_ref, b_ref, *refs):
    x_refs, (mod_ref, gpost_ref, o_ref) = refs[:n_x], refs[n_x:]
    y = _dot(a_ref[...], w_ref[...]) + b_ref[...]
    o_ref[...] = _post_residual(y, _stream_tile(x_refs), mod_ref[0], gpost_ref[...], 2)


def _mix_out(a, w, b, stream, mods, g_post, n_rows):
    tm = 256
    xs, x_specs = _stream_operands(stream, tm)
    return pl.pallas_call(
        functools.partial(_mix_out_kernel, len(xs)),
        grid=(n_rows // tm,),
        in_specs=[
            _row_spec(tm, D_MODEL),
            _resident((D_MODEL, D_MODEL)),
            _resident((1, D_MODEL)),
            *x_specs,
            _mod_spec(tm),
            _resident((1, D_MODEL)),
        ],
        out_specs=_row_spec(tm, D_MODEL),
        out_shape=jax.ShapeDtypeStruct((n_rows, D_MODEL), F32),
        compiler_params=_params(1),
        name="mix_out",
    )(a, w, b, *xs, mods, g_post)


def _rope_tables():
    f32 = np.float32
    t = np.arange(SEQ, dtype=np.int32)
    row = (t // GRID_W).astype(f32)
    col = (t % GRID_W).astype(f32)
    half = QK_ROPE_DIM // 2
    inv = (f32(ROPE_THETA) ** (-np.arange(0, half, 2, dtype=f32) / f32(half))).astype(f32)
    ang_r, ang_c = row[:, None] * inv, col[:, None] * inv
    cr, sr, cc, sc = np.cos(ang_r), np.sin(ang_r), np.cos(ang_c), np.sin(ang_c)
    zeros = np.zeros((SEQ, QK_ROPE_DIM), f32)
    cos_tab = np.concatenate([cr, cr, cc, cc, zeros], axis=1)
    sin_tab = np.concatenate([-sr, sr, -sc, sc, zeros], axis=1)
    ident = np.concatenate([np.ones((CTX_LEN, QK_ROPE_DIM), f32),
                            np.zeros((CTX_LEN, QK_ROPE_DIM), f32)], axis=1)
    cos_tab = np.concatenate([cos_tab, ident], axis=0)
    sin_tab = np.concatenate([sin_tab, np.zeros((CTX_LEN, LANES), f32)], axis=0)
    return jnp.asarray(cos_tab, F32), jnp.asarray(sin_tab, F32)


def _swap_half_lanes(w):
    perm = np.arange(QK_ROPE_DIM) ^ (QK_ROPE_DIM // 4)
    return w[..., perm]


def _mla_weights(w_dq, w_uq, w_dkv, w_ukv, w_o):
    uq = w_uq.reshape(Q_LORA_RANK, N_HEADS, QK_DIM)
    rope = uq[..., QK_NOPE_DIM:]
    uq = jnp.concatenate([uq[..., :QK_NOPE_DIM], rope, _swap_half_lanes(rope)], axis=-1)
    kpe = w_dkv[:, KV_LORA_RANK:]
    dkv = jnp.concatenate([w_dkv[:, :KV_LORA_RANK], kpe, _swap_half_lanes(kpe)], axis=-1)
    ukv = w_ukv.reshape(KV_LORA_RANK, N_HEADS, QK_NOPE_DIM + V_HEAD_DIM)
    uk_t = ukv[..., :QK_NOPE_DIM].reshape(KV_LORA_RANK, N_HEADS * QK_NOPE_DIM).T
    uv = ukv[..., QK_NOPE_DIM:].reshape(KV_LORA_RANK, N_HEADS * V_HEAD_DIM)
    return (w_dq.astype(BF16), uq.reshape(Q_LORA_RANK, N_HEADS * HEAD_PAD).astype(BF16),
            dkv.astype(BF16), uk_t.astype(BF16), uv.astype(BF16), w_o.astype(BF16))


def _rope(blk, cos_tab, sin_tab):
    return blk * cos_tab + pltpu.roll(blk, LANES // 2, axis=1) * sin_tab


def _mla_proj_kernel(n_x, *refs):
    x_refs = refs[:n_x]
    (mod_ref, gpre_ref, wdq_ref, qn_ref, wuq_ref, wdkv_ref, kvn_ref, wukt_ref, wuv_ref, cos_ref,
     sin_ref, q_ref, kt_ref, v_ref) = refs[n_x:]
    h = _premix(_stream_tile(x_refs), mod_ref[0], gpre_ref[...], 0).astype(BF16)
    cos_tab, sin_tab = cos_ref[...], sin_ref[...]
    q_lat = _rms(_dot(h, wdq_ref[...]), qn_ref[...]).astype(BF16)
    kv_a = _dot(h, wdkv_ref[...])
    c_kv = _rms(kv_a[:, :KV_LORA_RANK], kvn_ref[...]).astype(BF16)
    k_pe_t = _rope(kv_a[:, KV_LORA_RANK:], cos_tab, sin_tab).T.astype(BF16)
    k_nope_t = _qk(wukt_ref[...], c_kv)
    v_ref[...] = _dot(c_kv, wuv_ref[...]).astype(BF16)
    for hd in range(N_HEADS):
        lo = hd * HEAD_PAD
        q = _dot(q_lat, wuq_ref[:, lo:lo + HEAD_PAD])
        q_ref[:, lo:lo + LANES] = (q[:, :LANES] * ATTN_SCALE).astype(BF16)
        q_ref[:, lo + LANES:lo + HEAD_PAD] = (
            _rope(q[:, LANES:], cos_tab, sin_tab) * ATTN_SCALE).astype(BF16)
        kt_ref[lo:lo + LANES, :] = k_nope_t[hd * LANES:(hd + 1) * LANES].astype(BF16)
        kt_ref[lo + LANES:lo + HEAD_PAD, :] = k_pe_t


def _mla_proj(stream, mods, g_pre, w_dq, q_norm, w_uq, w_dkv, kv_norm, w_uk_t, w_uv, cos_tab,
              sin_tab):
    tm = 256
    xs, x_specs = _stream_operands(stream, tm)
    lat_tiles, seq_tiles = N_LAT // tm, SEQ // tm
    tab_spec = pl.BlockSpec(
        (tm, LANES), lambda i: (jnp.where(i < lat_tiles, i % seq_tiles, seq_tiles), 0))
    wide = N_HEADS * HEAD_PAD
    return pl.pallas_call(
        functools.partial(_mla_proj_kernel, len(xs)),
        grid=(N_TOK // tm,),
        in_specs=[
            *x_specs,
            _mod_spec(tm),
            _resident((1, D_MODEL)),
            _resident((D_MODEL, Q_LORA_RANK)),
            _resident((1, Q_LORA_RANK)),
            _resident((Q_LORA_RANK, wide)),
            _resident((D_MODEL, KV_LORA_RANK + LANES)),
            _resident((1, KV_LORA_RANK)),
            _resident((N_HEADS * QK_NOPE_DIM, KV_LORA_RANK)),
            _resident((KV_LORA_RANK, N_HEADS * V_HEAD_DIM)),
            tab_spec,
            tab_spec,
        ],
        out_specs=[_row_spec(tm, wide), pl.BlockSpec((wide, tm), lambda i: (0, i)),
                   _row_spec(tm, D_MODEL)],
        out_shape=[jax.ShapeDtypeStruct((N_TOK, wide), BF16),
                   jax.ShapeDtypeStruct((wide, N_TOK), BF16),
                   jax.ShapeDtypeStruct((N_TOK, D_MODEL), BF16)],
        compiler_params=_params(1),
        name="mla_proj",
    )(*xs, mods, g_pre, w_dq, q_norm, w_uq, w_dkv, kv_norm, w_uk_t, w_uv, cos_tab, sin_tab)


def _qk(q, k):
    return lax.dot_general(q, k, (((1,), (1,)), ((), ())), preferred_element_type=F32)


def _attn_kernel(tq, tk, q_ref, ktc_ref, ktl_ref, vc_ref, vl_ref, qn_ref, ktcn_ref, ktln_ref, o_ref,
                 s_even, s_odd, m_even, m_odd, vaug_scr):
    vaug_scr[0:CTX_LEN, 0:V_HEAD_DIM] = vc_ref[...]
    vaug_scr[CTX_LEN:, 0:V_HEAD_DIM] = vl_ref[...]
    vaug_scr[:, V_HEAD_DIM:] = jnp.ones((CTX_LEN + SEQ, LANES), BF16)

    chunks = [(0, CTX_LEN)] + [(CTX_LEN + c * tk, tk) for c in range(SEQ // tk)]
    n_tiles = SEQ // tq

    def rows(tile):
        start = tile * tq if isinstance(tile, int) else pl.multiple_of(tile * tq, tq)
        return pl.ds(start, tq)

    def scores(q, kc_ref, kl_ref, s_scr, m_scr):
        for n, (off, width) in enumerate(chunks):
            kt = kc_ref[...] if n == 0 else kl_ref[:, off - CTX_LEN:off - CTX_LEN + width]
            s = _dot(q, kt)
            s_scr[:, off:off + width] = s
            cm = s[:, 0:LANES]
            for t in range(1, width // LANES):
                cm = jnp.maximum(cm, s[:, t * LANES:(t + 1) * LANES])
            m_scr[...] = cm if n == 0 else jnp.maximum(m_scr[...], cm)

    def outputs(tile, s_scr, m_scr):
        m = jnp.max(m_scr[...], axis=1, keepdims=True)
        acc = None
        for off, width in chunks:
            p = jnp.exp(s_scr[:, off:off + width] - m).astype(BF16)
            part = _dot(p, vaug_scr[off:off + width, :])
            acc = part if acc is None else acc + part
        o_ref[rows(tile), :] = (acc[:, :V_HEAD_DIM] / acc[:, V_HEAD_DIM:]).astype(BF16)

    def own_scores(tile, s_scr, m_scr):
        scores(q_ref[rows(tile), :], ktc_ref, ktl_ref, s_scr, m_scr)

    @pl.when(pl.program_id(0) == 0)
    def _():
        own_scores(0, s_even, m_even)

    def tile_pair(j, carry):
        t = 2 * j
        own_scores(t + 1, s_odd, m_odd)
        outputs(t, s_even, m_even)
        own_scores(t + 2, s_even, m_even)
        outputs(t + 1, s_odd, m_odd)
        return carry

    lax.fori_loop(0, n_tiles // 2 - 1, tile_pair, 0)
    own_scores(n_tiles - 1, s_odd, m_odd)
    outputs(n_tiles - 2, s_even, m_even)
    scores(qn_ref[...], ktcn_ref, ktln_ref, s_even, m_even)
    outputs(n_tiles - 1, s_odd, m_odd)


def _attention(q, kt, v, n_out_rows):
    tq, tk = 512, 256
    ctx_blk = N_LAT // CTX_LEN
    n_keys = CTX_LEN + SEQ
    n_steps = BATCH * N_HEADS

    def this(s):
        return s // N_HEADS, s % N_HEADS

    def following(s):
        return this(jnp.minimum(s + 1, n_steps - 1))

    def spec(shape, pick, index):
        return pl.BlockSpec(shape, lambda s: index(*pick(s)))

    return pl.pallas_call(
        functools.partial(_attn_kernel, tq, tk),
        grid=(n_steps,),
        in_specs=[
            spec((SEQ, HEAD_PAD), this, lambda b, h: (b, h)),
            spec((HEAD_PAD, CTX_LEN), this, lambda b, h: (h, ctx_blk + b)),
            spec((HEAD_PAD, SEQ), this, lambda b, h: (h, b)),
            spec((CTX_LEN, V_HEAD_DIM), this, lambda b, h: (ctx_blk + b, h)),
            spec((SEQ, V_HEAD_DIM), this, lambda b, h: (b, h)),
            spec((tq, HEAD_PAD), following, lambda b, h: (b * (SEQ // tq), h)),
            spec((HEAD_PAD, CTX_LEN), following, lambda b, h: (h, ctx_blk + b)),
            spec((HEAD_PAD, SEQ), following, lambda b, h: (h, b)),
        ],
        out_specs=spec((SEQ, V_HEAD_DIM), this, lambda b, h: (b, h)),
        out_shape=jax.ShapeDtypeStruct((n_out_rows, N_HEADS * V_HEAD_DIM), BF16),
        scratch_shapes=[pltpu.VMEM((tq, n_keys), F32), pltpu.VMEM((tq, n_keys), F32),
                        pltpu.VMEM((tq, LANES), F32), pltpu.VMEM((tq, LANES), F32),
                        pltpu.VMEM((n_keys, V_HEAD_DIM + LANES), BF16)],
        compiler_params=_params(1),
        name="mla_attention",
    )(q, kt, kt, v, v, q, kt, kt)


def _attn_ctx_kernel(q_ref, kt_ref, v_ref, o_in_ref, o_ref):
    del o_in_ref
    s = _dot(q_ref[...], kt_ref[...])
    p = jnp.exp(s - jnp.max(s, axis=1, keepdims=True))
    l = jnp.sum(p, axis=1, keepdims=True)
    o_ref[...] = (_dot(p.astype(BF16), v_ref[...]) / l).astype(BF16)


def _attention_ctx(q, kt, v, o):
    ctx_blk = N_LAT // CTX_LEN
    q_spec = pl.BlockSpec((CTX_LEN, HEAD_PAD), lambda b, h: (ctx_blk + b, h))
    kt_spec = pl.BlockSpec((HEAD_PAD, CTX_LEN), lambda b, h: (h, ctx_blk + b))
    v_spec = pl.BlockSpec((CTX_LEN, V_HEAD_DIM), lambda b, h: (ctx_blk + b, h))
    return pl.pallas_call(
        _attn_ctx_kernel,
        grid=(BATCH, N_HEADS),
        in_specs=[q_spec, kt_spec, v_spec, pl.BlockSpec(memory_space=pl.ANY)],
        out_specs=v_spec,
        out_shape=jax.ShapeDtypeStruct(o.shape, o.dtype),
        input_output_aliases={3: 0},
        compiler_params=_params(2),
        name="mla_attention_ctx",
    )(q, kt, v, o)


def _conv_in_kernel(x_ref, mod_ref, gpre_ref, w_ref, b_ref, o_ref):
    h = _premix(x_ref[...], mod_ref[0], gpre_ref[...], 0).astype(BF16)
    a = _dot(h, w_ref[:, :D_MODEL]) + b_ref[:, :D_MODEL]
    gate = _dot(h, w_ref[:, D_MODEL:]) + b_ref[:, D_MODEL:]
    o_ref[...] = a * _sigmoid(gate)


def _conv_in(x, mods, g_pre, w, b):
    tm = 256
    return pl.pallas_call(
        _conv_in_kernel,
        grid=(N_TOK // tm,),
        in_specs=[
            _row_spec(tm, D_MODEL),
            _mod_spec(tm),
            _resident((1, D_MODEL)),
            _resident((D_MODEL, 2 * D_MODEL)),
            _resident((1, 2 * D_MODEL)),
        ],
        out_specs=_row_spec(tm, D_MODEL),
        out_shape=jax.ShapeDtypeStruct((N_TOK, D_MODEL), F32),
        compiler_params=_params(1),
        name="conv_in_glu",
    )(x, mods, g_pre, w, b)


def _conv_out_kernel(tm, g_ref, left_ref, right_ref, wdw_ref, bdw_ref, lng_ref, lnb_ref, w2_ref,
                     b2_ref, x_ref, mod_ref, gpost_ref, o_ref, pad_scr, y_scr, shift_scr, lhs_scr):
    i = pl.program_id(0)
    seg_tiles = SEQ // tm
    in_lat = i < N_LAT // tm
    has_left = jnp.logical_and(in_lat, i % seg_tiles != 0)
    has_right = jnp.logical_and(in_lat, i % seg_tiles != seg_tiles - 1)
    n_blocks = D_MODEL // LANES
    for cb in range(n_blocks):
        lanes = slice(cb * LANES, (cb + 1) * LANES)
        pad_scr[cb, 0:HALO, :] = jnp.where(has_left, left_ref[:, lanes], 0.0)
        pad_scr[cb, HALO:HALO + tm, :] = g_ref[:, lanes]
        pad_scr[cb, HALO + tm:HALO + tm + HALO, :] = jnp.where(has_right, right_ref[:, lanes], 0.0)

    rows, sub, first, span = CONV_ROWS, SUBLANES, CONV_FIRST, CONV_SPAN
    n_acc = rows // sub

    def channel_block(cb, carry):
        for r0 in range(0, tm, rows):
            acc = [jnp.broadcast_to(bdw_ref[cb], (sub, LANES))] * n_acc
            for res in range(sub):
                if res:
                    shift_scr[res] = pad_scr[cb, r0 + res:r0 + res + span, :]
                taps = [(off // sub, off - first) for off in range(res, first + CONV_WIDTH, sub)
                        if off >= first]
                w_taps = [jnp.broadcast_to(wdw_ref[cb, j:j + 1, :], (sub, LANES)) for _, j in taps]
                for k in range(span // sub):
                    lo = k * sub
                    v = (shift_scr[res, lo:lo + sub, :] if res
                         else pad_scr[cb, r0 + lo:r0 + lo + sub, :])
                    for (a, _), w_tap in zip(taps, w_taps):
                        if 0 <= k - a < n_acc:
                            acc[k - a] = acc[k - a] + w_tap * v
            for n in range(n_acc):
                y_scr[cb, r0 + n * sub:r0 + (n + 1) * sub, :] = acc[n]
        return carry

    lax.fori_loop(0, n_blocks, channel_block, 0)

    y = y_scr[...]
    mu = jnp.sum(jnp.sum(y, axis=0), axis=-1, keepdims=True) * (1.0 / D_MODEL)
    yc = y - mu
    var = jnp.sum(jnp.sum(yc * yc, axis=0), axis=-1, keepdims=True) * (1.0 / D_MODEL)
    z = yc * lax.rsqrt(var + EPS) * lng_ref[...] + lnb_ref[...]
    z = z * _sigmoid(z)
    for cb in range(n_blocks):
        lhs_scr[:, cb * LANES:(cb + 1) * LANES] = z[cb].astype(BF16)
    out = _dot(lhs_scr[...], w2_ref[...]) + b2_ref[...]
    o_ref[...] = _post_residual(out, x_ref[...], mod_ref[0], gpost_ref[...], 2)


def _conv_out(g, w_dw, b_dw, ln_g, ln_b, w2, b2, x, mods, g_post, n_rows):
    tm = 256
    halo_per_tile = tm // HALO
    last_halo = N_TOK // HALO - 1
    n_blocks = D_MODEL // LANES

    def by_block(v):
        return v.reshape(v.shape[0], n_blocks, LANES).swapaxes(0, 1)

    return pl.pallas_call(
        functools.partial(_conv_out_kernel, tm),
        grid=(n_rows // tm,),
        in_specs=[
            _row_spec(tm, D_MODEL),
            pl.BlockSpec((HALO, D_MODEL), lambda i: (jnp.maximum(i * halo_per_tile - 1, 0), 0)),
            pl.BlockSpec((HALO, D_MODEL),
                         lambda i: (jnp.minimum((i + 1) * halo_per_tile, last_halo), 0)),
            _resident((n_blocks, CONV_WIDTH, LANES)),
            _resident((n_blocks, 1, LANES)),
            _resident((n_blocks, 1, LANES)),
            _resident((n_blocks, 1, LANES)),
            _resident((D_MODEL, D_MODEL)),
            _resident((1, D_MODEL)),
            _row_spec(tm, D_MODEL),
            _mod_spec(tm),
            _resident((1, D_MODEL)),
        ],
        out_specs=_row_spec(tm, D_MODEL),
        out_shape=jax.ShapeDtypeStruct((n_rows, D_MODEL), F32),
        scratch_shapes=[pltpu.VMEM((n_blocks, tm + 2 * HALO, LANES), F32),
                        pltpu.VMEM((n_blocks, tm, LANES), F32),
                        pltpu.VMEM((SUBLANES, CONV_SPAN, LANES), F32),
                        pltpu.VMEM((tm, D_MODEL), BF16)],
        compiler_params=_params(1),
        name="conv_dw_out",
    )(g, g, g, by_block(w_dw), by_block(b_dw), by_block(ln_g), by_block(ln_b), w2, b2, x, mods,
      g_post)


def _gmlp_in_kernel(x_ref, mod_ref, gpre_ref, w_ref, b_ref, lng_ref, lnb_ref, u_ref, v_ref):
    h = _premix(x_ref[...], mod_ref[0], gpre_ref[...], 0).astype(BF16)
    u_ref[...] = _gelu(_dot(h, w_ref[:, :D_MODEL]) + b_ref[:, :D_MODEL])
    v = _gelu(_dot(h, w_ref[:, D_MODEL:]) + b_ref[:, D_MODEL:])
    v_ref[...] = _layer_norm(v, lng_ref[...], lnb_ref[...]).astype(BF16)


def _gmlp_in(x, mods, g_pre, w_uv, b_uv, ln_g, ln_b):
    tm = 256
    return pl.pallas_call(
        _gmlp_in_kernel,
        grid=(N_TOK // tm,),
        in_specs=[
            _row_spec(tm, D_MODEL),
            _mod_spec(tm),
            _resident((1, D_MODEL)),
            _resident((D_MODEL, 2 * D_MODEL)),
            _resident((1, 2 * D_MODEL)),
            _resident((1, D_MODEL)),
            _resident((1, D_MODEL)),
        ],
        out_specs=[_row_spec(tm, D_MODEL), _row_spec(tm, D_MODEL)],
        out_shape=[jax.ShapeDtypeStruct((N_TOK, D_MODEL), F32),
                   jax.ShapeDtypeStruct((N_TOK, D_MODEL), BF16)],
        compiler_params=_params(1),
        name="gmlp_in",
    )(x, mods, g_pre, w_uv, b_uv, ln_g, ln_b)


def _gmlp_out_kernel(tm, u_ref, v_ref, ws_ref, bs_ref, w_ref, b_ref, x_ref, mod_ref, gpost_ref,
                     o_ref, t_scr):
    for g in range(N_GROUPS):
        c0 = g * GROUP_DIM
        w_s = ws_ref[g]
        for r0 in range(0, tm, CHUNK):
            sv = _dot(w_s, v_ref[r0:r0 + CHUNK, c0:c0 + GROUP_DIM]) + bs_ref[:, c0:c0 + GROUP_DIM]
            t_scr[r0:r0 + CHUNK, c0:c0 + GROUP_DIM] = (
                u_ref[r0:r0 + CHUNK, c0:c0 + GROUP_DIM] * sv).astype(BF16)
    out = _dot(t_scr[...], w_ref[...]) + b_ref[...]
    o_ref[...] = _post_residual(out, x_ref[...], mod_ref[0], gpost_ref[...], 2)


def _gmlp_out(u, v, w_s, b_s_wide, w, b, x, mods, g_post, n_rows):
    tm = 256
    return pl.pallas_call(
        functools.partial(_gmlp_out_kernel, tm),
        grid=(n_rows // tm,),
        in_specs=[
            _row_spec(tm, D_MODEL),
            _row_spec(tm, D_MODEL),
            _resident((N_GROUPS, CHUNK, CHUNK)),
            _resident((CHUNK, D_MODEL)),
            _resident((D_MODEL, D_MODEL)),
            _resident((1, D_MODEL)),
            _row_spec(tm, D_MODEL),
            _mod_spec(tm),
            _resident((1, D_MODEL)),
        ],
        out_specs=_row_spec(tm, D_MODEL),
        out_shape=jax.ShapeDtypeStruct((n_rows, D_MODEL), F32),
        scratch_shapes=[pltpu.VMEM((tm, D_MODEL), BF16)],
        compiler_params=_params(1),
        name="gmlp_spatial_out",
    )(u, v, w_s, b_s_wide, w, b, x, mods, g_post)


def _row(v):
    return v.reshape(1, -1)


def kernel(x, c, ctx, c_ctx, ada_w, ada_b, norm_mix_pre, norm_mix_post, norm_ffn_pre, norm_ffn_post, mla_w_dq, mla_q_norm, mla_w_uq, mla_w_dkv, mla_kv_norm, mla_w_ukv, mla_w_o, conv_w_pw1, conv_b_pw1, conv_w_dw, conv_b_dw, conv_ln_g, conv_ln_b, conv_w_pw2, conv_b_pw2, gmlp_w_uv, gmlp_b_uv, gmlp_ln_g, gmlp_ln_b, gmlp_w_s, gmlp_b_s, gmlp_w_out, gmlp_b_out, ffn_w1, ffn_w2):
    stream = (x.reshape(N_LAT, D_MODEL), ctx.reshape(N_CTX, D_MODEL))
    cond = jnp.concatenate([c, c_ctx[None, :], jnp.zeros((8 - BATCH - 1, D_MODEL), F32)], axis=0)
    mods_all = _ada_all(cond, ada_w, ada_b)[:, :N_SETS].reshape(DEPTH, N_SETS, N_MOD, D_MODEL)
    cos_tab, sin_tab = _rope_tables()
    zero_bias = jnp.zeros((1, D_MODEL), F32)
    ffn_w1_bf16, ffn_w2_bf16 = ffn_w1.astype(BF16), ffn_w2.astype(BF16)

    for i in range(DEPTH):
        last = i == DEPTH - 1
        kind, j = i % N_MIXERS, i // N_MIXERS
        n_rows = N_LAT if last else N_TOK
        mods = mods_all[i]
        if kind == 0:
            w_dq, w_uq, w_dkv, w_uk_t, w_uv, w_o = _mla_weights(
                mla_w_dq[j], mla_w_uq[j], mla_w_dkv[j], mla_w_ukv[j], mla_w_o[j])
            q, kt, v = _mla_proj(stream, mods, _row(norm_mix_pre[i]), w_dq, _row(mla_q_norm[j]),
                                 w_uq, w_dkv, _row(mla_kv_norm[j]), w_uk_t, w_uv, cos_tab, sin_tab)
            o = _attention(q, kt, v, n_rows)
            if not last:
                o = _attention_ctx(q, kt, v, o)
            stream = _mix_out(o, w_o, zero_bias, stream, mods, _row(norm_mix_post[i]), n_rows)
        elif kind == 1:
            g = _conv_in(stream, mods, _row(norm_mix_pre[i]), conv_w_pw1[j].astype(BF16),
                         _row(conv_b_pw1[j]))
            stream = _conv_out(g, conv_w_dw[j], _row(conv_b_dw[j]), _row(conv_ln_g[j]),
                               _row(conv_ln_b[j]), conv_w_pw2[j].astype(BF16),
                               _row(conv_b_pw2[j]), stream, mods, _row(norm_mix_post[i]), n_rows)
        else:
            u, v = _gmlp_in(stream, mods, _row(norm_mix_pre[i]), gmlp_w_uv[j].astype(BF16),
                            _row(gmlp_b_uv[j]), _row(gmlp_ln_g[j]), _row(gmlp_ln_b[j]))
            b_s_wide = jnp.repeat(gmlp_b_s[j].T, GROUP_DIM, axis=1)
            stream = _gmlp_out(u, v, gmlp_w_s[j].astype(BF16), b_s_wide,
                               gmlp_w_out[j].astype(BF16), _row(gmlp_b_out[j]), stream, mods,
                               _row(norm_mix_post[i]), n_rows)
        stream = _ffn(stream, mods, _row(norm_ffn_pre[i]), _row(norm_ffn_post[i]),
                      ffn_w1_bf16, ffn_w2_bf16, i, n_rows)
    return stream.reshape(BATCH, SEQ, D_MODEL)
```

```python
import functools

import jax
import jax.numpy as jnp
import numpy as np
from jax import lax
from jax.experimental import pallas as pl
from jax.experimental.pallas import tpu as pltpu

D_MODEL = 2048
BATCH = 2
SEQ = 4096
DEPTH = 4
CTX_LEN = 256
GRID_W = 64
N_MIXERS = 3
N_HEADS = 16
QK_NOPE_DIM = 128
QK_ROPE_DIM = 64
QK_DIM = QK_NOPE_DIM + QK_ROPE_DIM
V_HEAD_DIM = 128
Q_LORA_RANK = 768
KV_LORA_RANK = 512
ROPE_THETA = 10000.0
ATTN_SCALE = QK_DIM ** -0.5
CONV_WIDTH = 31
CONV_PAD = CONV_WIDTH // 2
CHUNK = 128
N_GROUPS = 16
GROUP_DIM = D_MODEL // N_GROUPS
D_FF = 4 * D_MODEL
N_MOD = 6
EPS = 1e-6

N_LAT = BATCH * SEQ
N_CTX = BATCH * CTX_LEN
N_TOK = N_LAT + N_CTX
N_SETS = 3
HEAD_PAD = 256
LANES = 128
SUBLANES = 8
HALO = 16
CONV_ROWS = 128
CONV_FIRST = HALO - CONV_PAD
CONV_SPAN = CONV_ROWS + SUBLANES * ((CONV_FIRST + CONV_WIDTH - 1) // SUBLANES)
VMEM_LIMIT = 56 * 1024 * 1024
ROW_CHUNK = 16

F32 = jnp.float32
BF16 = jnp.bfloat16


def _dot(a, b):
    return jnp.dot(a, b, preferred_element_type=F32)


def _rms(x, g):
    return x * lax.rsqrt(jnp.mean(x * x, axis=-1, keepdims=True) + EPS) * g


def _layer_norm(x, g, b):
    mu = jnp.mean(x, axis=-1, keepdims=True)
    xc = x - mu
    var = jnp.mean(xc * xc, axis=-1, keepdims=True)
    return xc * lax.rsqrt(var + EPS) * g + b


def _sigmoid(x):
    return 1.0 / (1.0 + jnp.exp(-x))


def _gelu(x):
    return 0.5 * x * (1.0 + lax.erf(x * np.float32(np.sqrt(0.5))))


def _premix(x, mod, g, shift_idx):
    return _rms(x, g) * (1.0 + mod[shift_idx + 1:shift_idx + 2]) + mod[shift_idx:shift_idx + 1]


def _params(n_grid_axes):
    return pltpu.CompilerParams(
        dimension_semantics=("arbitrary",) * n_grid_axes,
        vmem_limit_bytes=VMEM_LIMIT)


def _resident(shape):
    nd = len(shape)
    return pl.BlockSpec(shape, lambda *_: (0,) * nd, pipeline_mode=pl.Buffered(1))


def _mod_spec(tm):
    return pl.BlockSpec((1, N_MOD, D_MODEL),
                        lambda i, *_: (jnp.minimum(i * tm // SEQ, N_SETS - 1), 0, 0))


def _row_spec(tm, width):
    return pl.BlockSpec((tm, width), lambda i, *_: (i, 0))


def _stream_operands(stream, tm):
    if not isinstance(stream, tuple):
        return (stream,), (_row_spec(tm, D_MODEL),)
    lat_tiles = N_LAT // tm
    return stream, (
        pl.BlockSpec((tm, D_MODEL), lambda i, *_: (jnp.minimum(i, lat_tiles - 1), 0)),
        pl.BlockSpec((tm, D_MODEL), lambda i, *_: (jnp.maximum(i - lat_tiles, 0), 0)))


def _stream_tile(x_refs):
    if len(x_refs) == 1:
        return x_refs[0][...]
    lat_ref, ctx_ref = x_refs
    return jnp.where(pl.program_id(0) < N_LAT // lat_ref.shape[0], lat_ref[...], ctx_ref[...])


def _ada_kernel(cond_ref, w_ref, b_ref, o_ref):
    c = cond_ref[...]
    s = (c * _sigmoid(c)).astype(BF16)
    o_ref[0] = _dot(s, w_ref[0].astype(BF16)) + b_ref[0]


def _ada_all(cond, ada_w, ada_b):
    tn = 1024
    n_out = N_MOD * D_MODEL
    return pl.pallas_call(
        _ada_kernel,
        grid=(DEPTH, n_out // tn),
        in_specs=[
            pl.BlockSpec((8, D_MODEL), lambda l, j: (0, 0)),
            pl.BlockSpec((1, D_MODEL, tn), lambda l, j: (l, 0, j)),
            pl.BlockSpec((1, 1, tn), lambda l, j: (l, 0, j)),
        ],
        out_specs=pl.BlockSpec((1, 8, tn), lambda l, j: (l, 0, j)),
        out_shape=jax.ShapeDtypeStruct((DEPTH, 8, n_out), F32),
        compiler_params=_params(2),
        name="ada_mod",
    )(cond, ada_w, ada_b.reshape(DEPTH, 1, n_out))


def _post_residual(y, x, mod, g_post, gate_idx):
    return x + mod[gate_idx:gate_idx + 1] * _rms(y, g_post)


def _ffn_kernel(n_f, n_sub, x_ref, mod_ref, gpre_ref, gpost_ref, w1_ref, w2_ref, o_ref, h_scr):
    f = pl.program_id(1)

    n_rows = h_scr.shape[0]
    sub_width = w1_ref.shape[2] // n_sub

    @pl.when(f == 0)
    def _():
        mod = mod_ref[0]
        gain = gpre_ref[...] * (1.0 + mod[4:5])
        shift = mod[3:4]

        for r0 in range(0, n_rows, ROW_CHUNK):
            x = x_ref[r0:r0 + ROW_CHUNK, :]
            inv = lax.rsqrt(jnp.mean(x * x, axis=-1, keepdims=True) + EPS)
            h_scr[r0:r0 + ROW_CHUNK, :] = (x * inv * gain + shift).astype(BF16)
        o_ref[...] = jnp.zeros(o_ref.shape, F32)

    for s in range(n_sub):
        cols = slice(s * sub_width, (s + 1) * sub_width)
        a = _dot(h_scr[...], w1_ref[0, :, cols])
        a = jnp.square(jnp.maximum(a, 0.0)).astype(BF16)
        o_ref[...] += _dot(a, w2_ref[0, cols, :])

    @pl.when(f == n_f - 1)
    def _():
        gain = mod_ref[0][5:6] * gpost_ref[...]

        for r0 in range(0, n_rows, ROW_CHUNK):
            y = o_ref[r0:r0 + ROW_CHUNK, :]
            inv = lax.rsqrt(jnp.mean(y * y, axis=-1, keepdims=True) + EPS)
            o_ref[r0:r0 + ROW_CHUNK, :] = x_ref[r0:r0 + ROW_CHUNK, :] + y * inv * gain


def _ffn(x, mods, g_pre, g_post, w1, w2, layer, n_rows):
    tm, tf, n_sub = 512, 2048, 2
    n_f = D_FF // tf
    return pl.pallas_call(
        functools.partial(_ffn_kernel, n_f, n_sub),
        grid=(n_rows // tm, n_f),
        in_specs=[
            _row_spec(tm, D_MODEL),
            _mod_spec(tm),
            pl.BlockSpec((1, D_MODEL), lambda i, f: (0, 0)),
            pl.BlockSpec((1, D_MODEL), lambda i, f: (0, 0)),
            pl.BlockSpec((1, D_MODEL, tf), lambda i, f: (layer, 0, f)),
            pl.BlockSpec((1, tf, D_MODEL), lambda i, f: (layer, f, 0)),
        ],
        out_specs=_row_spec(tm, D_MODEL),
        out_shape=jax.ShapeDtypeStruct((n_rows, D_MODEL), F32),
        scratch_shapes=[pltpu.VMEM((tm, D_MODEL), BF16)],
        compiler_params=_params(2),
        name="ffn",
    )(x, mods, g_pre, g_post, w1, w2)


def _mix_out_kernel(n_x, a_ref, w_ref, b_ref, *refs):
    x_refs, (mod_ref, gpost_ref, o_ref) = refs[:n_x], refs[n_x:]
    y = _dot(a_ref[...], w_ref[...]) + b_ref[...]
    o_ref[...] = _post_residual(y, _stream_tile(x_refs), mod_ref[0], gpost_ref[...], 2)


def _mix_out(a, w, b, stream, mods, g_post, n_rows):
    tm = 256
    xs, x_specs = _stream_operands(stream, tm)
    return pl.pallas_call(
        functools.partial(_mix_out_kernel, len(xs)),
        grid=(n_rows // tm,),
        in_specs=[
            _row_spec(tm, D_MODEL),
            _resident((D_MODEL, D_MODEL)),
            _resident((1, D_MODEL)),
            *x_specs,
            _mod_spec(tm),
            _resident((1, D_MODEL)),
        ],
        out_specs=_row_spec(tm, D_MODEL),
        out_shape=jax.ShapeDtypeStruct((n_rows, D_MODEL), F32),
        compiler_params=_params(1),
        name="mix_out",
    )(a, w, b, *xs, mods, g_post)


def _rope_tables():
    f32 = np.float32
    t = np.arange(SEQ, dtype=np.int32)
    row = (t // GRID_W).astype(f32)
    col = (t % GRID_W).astype(f32)
    half = QK_ROPE_DIM // 2
    inv = (f32(ROPE_THETA) ** (-np.arange(0, half, 2, dtype=f32) / f32(half))).astype(f32)
    ang_r, ang_c = row[:, None] * inv, col[:, None] * inv
    cr, sr, cc, sc = np.cos(ang_r), np.sin(ang_r), np.cos(ang_c), np.sin(ang_c)
    zeros = np.zeros((SEQ, QK_ROPE_DIM), f32)
    cos_tab = np.concatenate([cr, cr, cc, cc, zeros], axis=1)
    sin_tab = np.concatenate([-sr, sr, -sc, sc, zeros], axis=1)
    ident = np.concatenate([np.ones((CTX_LEN, QK_ROPE_DIM), f32),
                            np.zeros((CTX_LEN, QK_ROPE_DIM), f32)], axis=1)
    cos_tab = np.concatenate([cos_tab, ident], axis=0)
    sin_tab = np.concatenate([sin_tab, np.zeros((CTX_LEN, LANES), f32)], axis=0)
    return jnp.asarray(cos_tab, F32), jnp.asarray(sin_tab, F32)


def _swap_half_lanes(w):
    perm = np.arange(QK_ROPE_DIM) ^ (QK_ROPE_DIM // 4)
    return w[..., perm]


def _mla_weights(w_dq, w_uq, w_dkv, w_ukv, w_o):
    uq = w_uq.reshape(Q_LORA_RANK, N_HEADS, QK_DIM)
    rope = uq[..., QK_NOPE_DIM:]
    uq = jnp.concatenate([uq[..., :QK_NOPE_DIM], rope, _swap_half_lanes(rope)], axis=-1)
    kpe = w_dkv[:, KV_LORA_RANK:]
    dkv = jnp.concatenate([w_dkv[:, :KV_LORA_RANK], kpe, _swap_half_lanes(kpe)], axis=-1)
    ukv = w_ukv.reshape(KV_LORA_RANK, N_HEADS, QK_NOPE_DIM + V_HEAD_DIM)
    uk_t = ukv[..., :QK_NOPE_DIM].reshape(KV_LORA_RANK, N_HEADS * QK_NOPE_DIM).T
    uv = ukv[..., QK_NOPE_DIM:].reshape(KV_LORA_RANK, N_HEADS * V_HEAD_DIM)
    return (w_dq.astype(BF16), uq.reshape(Q_LORA_RANK, N_HEADS * HEAD_PAD).astype(BF16),
            dkv.astype(BF16), uk_t.astype(BF16), uv.astype(BF16), w_o.astype(BF16))


def _rope(blk, cos_tab, sin_tab):
    return blk * cos_tab + pltpu.roll(blk, LANES // 2, axis=1) * sin_tab


def _mla_proj_kernel(n_x, *refs):
    x_refs = refs[:n_x]
    (mod_ref, gpre_ref, wdq_ref, qn_ref, wuq_ref, wdkv_ref, kvn_ref, wukt_ref, wuv_ref, cos_ref,
     sin_ref, q_ref, kt_ref, v_ref) = refs[n_x:]
    h = _premix(_stream_tile(x_refs), mod_ref[0], gpre_ref[...], 0).astype(BF16)
    cos_tab, sin_tab = cos_ref[...], sin_ref[...]
    q_lat = _rms(_dot(h, wdq_ref[...]), qn_ref[...]).astype(BF16)
    kv_a = _dot(h, wdkv_ref[...])
    c_kv = _rms(kv_a[:, :KV_LORA_RANK], kvn_ref[...]).astype(BF16)
    k_pe_t = _rope(kv_a[:, KV_LORA_RANK:], cos_tab, sin_tab).T.astype(BF16)
    k_nope_t = _qk(wukt_ref[...], c_kv)
    v_ref[...] = _dot(c_kv, wuv_ref[...]).astype(BF16)
    for hd in range(N_HEADS):
        lo = hd * HEAD_PAD
        q = _dot(q_lat, wuq_ref[:, lo:lo + HEAD_PAD])
        q_ref[:, lo:lo + LANES] = (q[:, :LANES] * ATTN_SCALE).astype(BF16)
        q_ref[:, lo + LANES:lo + HEAD_PAD] = (
            _rope(q[:, LANES:], cos_tab, sin_tab) * ATTN_SCALE).astype(BF16)
        kt_ref[lo:lo + LANES, :] = k_nope_t[hd * LANES:(hd + 1) * LANES].astype(BF16)
        kt_ref[lo + LANES:lo + HEAD_PAD, :] = k_pe_t


def _mla_proj(stream, mods, g_pre, w_dq, q_norm, w_uq, w_dkv, kv_norm, w_uk_t, w_uv, cos_tab,
              sin_tab):
    tm = 256
    xs, x_specs = _stream_operands(stream, tm)
    lat_tiles, seq_tiles = N_LAT // tm, SEQ // tm
    tab_spec = pl.BlockSpec(
        (tm, LANES), lambda i: (jnp.where(i < lat_tiles, i % seq_tiles, seq_tiles), 0))
    wide = N_HEADS * HEAD_PAD
    return pl.pallas_call(
        functools.partial(_mla_proj_kernel, len(xs)),
        grid=(N_TOK // tm,),
        in_specs=[
            *x_specs,
            _mod_spec(tm),
            _resident((1, D_MODEL)),
            _resident((D_MODEL, Q_LORA_RANK)),
            _resident((1, Q_LORA_RANK)),
            _resident((Q_LORA_RANK, wide)),
            _resident((D_MODEL, KV_LORA_RANK + LANES)),
            _resident((1, KV_LORA_RANK)),
            _resident((N_HEADS * QK_NOPE_DIM, KV_LORA_RANK)),
            _resident((KV_LORA_RANK, N_HEADS * V_HEAD_DIM)),
            tab_spec,
            tab_spec,
        ],
        out_specs=[_row_spec(tm, wide), pl.BlockSpec((wide, tm), lambda i: (0, i)),
                   _row_spec(tm, D_MODEL)],
        out_shape=[jax.ShapeDtypeStruct((N_TOK, wide), BF16),
                   jax.ShapeDtypeStruct((wide, N_TOK), BF16),
                   jax.ShapeDtypeStruct((N_TOK, D_MODEL), BF16)],
        compiler_params=_params(1),
        name="mla_proj",
    )(*xs, mods, g_pre, w_dq, q_norm, w_uq, w_dkv, kv_norm, w_uk_t, w_uv, cos_tab, sin_tab)


def _qk(q, k):
    return lax.dot_general(q, k, (((1,), (1,)), ((), ())), preferred_element_type=F32)


def _attn_kernel(tq, tk, q_ref, ktc_ref, ktl_ref, vc_ref, vl_ref, qn_ref, ktcn_ref, ktln_ref, o_ref,
                 s_even, s_odd, m_even, m_odd, vaug_scr):
    vaug_scr[0:CTX_LEN, 0:V_HEAD_DIM] = vc_ref[...]
    vaug_scr[CTX_LEN:, 0:V_HEAD_DIM] = vl_ref[...]
    vaug_scr[:, V_HEAD_DIM:] = jnp.ones((CTX_LEN + SEQ, LANES), BF16)

    chunks = [(0, CTX_LEN)] + [(CTX_LEN + c * tk, tk) for c in range(SEQ // tk)]
    n_tiles = SEQ // tq

    def rows(tile):
        start = tile * tq if isinstance(tile, int) else pl.multiple_of(tile * tq, tq)
        return pl.ds(start, tq)

    def scores(q, kc_ref, kl_ref, s_scr, m_scr):
        for n, (off, width) in enumerate(chunks):
            kt = kc_ref[...] if n == 0 else kl_ref[:, off - CTX_LEN:off - CTX_LEN + width]
            s = _dot(q, kt)
            s_scr[:, off:off + width] = s
            cm = s[:, 0:LANES]
            for t in range(1, width // LANES):
                cm = jnp.maximum(cm, s[:, t * LANES:(t + 1) * LANES])
            m_scr[...] = cm if n == 0 else jnp.maximum(m_scr[...], cm)

    def outputs(tile, s_scr, m_scr):
        m = jnp.max(m_scr[...], axis=1, keepdims=True)
        acc = None
        for off, width in chunks:
            p = jnp.exp(s_scr[:, off:off + width] - m).astype(BF16)
            part = _dot(p, vaug_scr[off:off + width, :])
            acc = part if acc is None else acc + part
        o_ref[rows(tile), :] = (acc[:, :V_HEAD_DIM] / acc[:, V_HEAD_DIM:]).astype(BF16)

    def own_scores(tile, s_scr, m_scr):
        scores(q_ref[rows(tile), :], ktc_ref, ktl_ref, s_scr, m_scr)

    @pl.when(pl.program_id(0) == 0)
    def _():
        own_scores(0, s_even, m_even)

    def tile_pair(j, carry):
        t = 2 * j
        own_scores(t + 1, s_odd, m_odd)
        outputs(t, s_even, m_even)
        own_scores(t + 2, s_even, m_even)
        outputs(t + 1, s_odd, m_odd)
        return carry

    lax.fori_loop(0, n_tiles // 2 - 1, tile_pair, 0)
    own_scores(n_tiles - 1, s_odd, m_odd)
    outputs(n_tiles - 2, s_even, m_even)
    scores(qn_ref[...], ktcn_ref, ktln_ref, s_even, m_even)
    outputs(n_tiles - 1, s_odd, m_odd)


def _attention(q, kt, v, n_out_rows):
    tq, tk = 512, 256
    ctx_blk = N_LAT // CTX_LEN
    n_keys = CTX_LEN + SEQ
    n_steps = BATCH * N_HEADS

    def this(s):
        return s // N_HEADS, s % N_HEADS

    def following(s):
        return this(jnp.minimum(s + 1, n_steps - 1))

    def spec(shape, pick, index):
        return pl.BlockSpec(shape, lambda s: index(*pick(s)))

    return pl.pallas_call(
        functools.partial(_attn_kernel, tq, tk),
        grid=(n_steps,),
        in_specs=[
            spec((SEQ, HEAD_PAD), this, lambda b, h: (b, h)),
            spec((HEAD_PAD, CTX_LEN), this, lambda b, h: (h, ctx_blk + b)),
            spec((HEAD_PAD, SEQ), this, lambda b, h: (h, b)),
            spec((CTX_LEN, V_HEAD_DIM), this, lambda b, h: (ctx_blk + b, h)),
            spec((SEQ, V_HEAD_DIM), this, lambda b, h: (b, h)),
            spec((tq, HEAD_PAD), following, lambda b, h: (b * (SEQ // tq), h)),
            spec((HEAD_PAD, CTX_LEN), following, lambda b, h: (h, ctx_blk + b)),
            spec((HEAD_PAD, SEQ), following, lambda b, h: (h, b)),
        ],
        out_specs=spec((SEQ, V_HEAD_DIM), this, lambda b, h: (b, h)),
        out_shape=jax.ShapeDtypeStruct((n_out_rows, N_HEADS * V_HEAD_DIM), BF16),
        scratch_shapes=[pltpu.VMEM((tq, n_keys), F32), pltpu.VMEM((tq, n_keys), F32),
                        pltpu.VMEM((tq, LANES), F32), pltpu.VMEM((tq, LANES), F32),
                        pltpu.VMEM((n_keys, V_HEAD_DIM + LANES), BF16)],
        compiler_params=_params(1),
        name="mla_attention",
    )(q, kt, kt, v, v, q, kt, kt)


def _attn_ctx_kernel(q_ref, kt_ref, v_ref, o_in_ref, o_ref):
    del o_in_ref
    s = _dot(q_ref[...], kt_ref[...])
    p = jnp.exp(s - jnp.max(s, axis=1, keepdims=True))
    l = jnp.sum(p, axis=1, keepdims=True)
    o_ref[...] = (_dot(p.astype(BF16), v_ref[...]) / l).astype(BF16)


def _attention_ctx(q, kt, v, o):
    ctx_blk = N_LAT // CTX_LEN
    q_spec = pl.BlockSpec((CTX_LEN, HEAD_PAD), lambda b, h: (ctx_blk + b, h))
    kt_spec = pl.BlockSpec((HEAD_PAD, CTX_LEN), lambda b, h: (h, ctx_blk + b))
    v_spec = pl.BlockSpec((CTX_LEN, V_HEAD_DIM), lambda b, h: (ctx_blk + b, h))
    return pl.pallas_call(
        _attn_ctx_kernel,
        grid=(BATCH, N_HEADS),
        in_specs=[q_spec, kt_spec, v_spec, pl.BlockSpec(memory_space=pl.ANY)],
        out_specs=v_spec,
        out_shape=jax.ShapeDtypeStruct(o.shape, o.dtype),
        input_output_aliases={3: 0},
        compiler_params=_params(2),
        name="mla_attention_ctx",
    )(q, kt, v, o)


def _conv_in_kernel(x_ref, mod_ref, gpre_ref, w_ref, b_ref, o_ref):
    h = _premix(x_ref[...], mod_ref[0], gpre_ref[...], 0).astype(BF16)
    a = _dot(h, w_ref[:, :D_MODEL]) + b_ref[:, :D_MODEL]
    gate = _dot(h, w_ref[:, D_MODEL:]) + b_ref[:, D_MODEL:]
    o_ref[...] = a * _sigmoid(gate)


def _conv_in(x, mods, g_pre, w, b):
    tm = 256
    return pl.pallas_call(
        _conv_in_kernel,
        grid=(N_TOK // tm,),
        in_specs=[
            _row_spec(tm, D_MODEL),
            _mod_spec(tm),
            _resident((1, D_MODEL)),
            _resident((D_MODEL, 2 * D_MODEL)),
            _resident((1, 2 * D_MODEL)),
        ],
        out_specs=_row_spec(tm, D_MODEL),
        out_shape=jax.ShapeDtypeStruct((N_TOK, D_MODEL), F32),
        compiler_params=_params(1),
        name="conv_in_glu",
    )(x, mods, g_pre, w, b)


def _conv_out_kernel(tm, g_ref, left_ref, right_ref, wdw_ref, bdw_ref, lng_ref, lnb_ref, w2_ref,
                     b2_ref, x_ref, mod_ref, gpost_ref, o_ref, pad_scr, y_scr, shift_scr, lhs_scr):
    i = pl.program_id(0)
    seg_tiles = SEQ // tm
    in_lat = i < N_LAT // tm
    has_left = jnp.logical_and(in_lat, i % seg_tiles != 0)
    has_right = jnp.logical_and(in_lat, i % seg_tiles != seg_tiles - 1)
    n_blocks = D_MODEL // LANES
    for cb in range(n_blocks):
        lanes = slice(cb * LANES, (cb + 1) * LANES)
        pad_scr[cb, 0:HALO, :] = jnp.where(has_left, left_ref[:, lanes], 0.0)
        pad_scr[cb, HALO:HALO + tm, :] = g_ref[:, lanes]
        pad_scr[cb, HALO + tm:HALO + tm + HALO, :] = jnp.where(has_right, right_ref[:, lanes], 0.0)

    rows, sub, first, span = CONV_ROWS, SUBLANES, CONV_FIRST, CONV_SPAN
    n_acc = rows // sub

    def channel_block(cb, carry):
        for r0 in range(0, tm, rows):
            acc = [jnp.broadcast_to(bdw_ref[cb], (sub, LANES))] * n_acc
            for res in range(sub):
                if res:
                    shift_scr[res] = pad_scr[cb, r0 + res:r0 + res + span, :]
                taps = [(off // sub, off - first) for off in range(res, first + CONV_WIDTH, sub)
                        if off >= first]
                w_taps = [jnp.broadcast_to(wdw_ref[cb, j:j + 1, :], (sub, LANES)) for _, j in taps]
                for k in range(span // sub):
                    lo = k * sub
                    v = (shift_scr[res, lo:lo + sub, :] if res
                         else pad_scr[cb, r0 + lo:r0 + lo + sub, :])
                    for (a, _), w_tap in zip(taps, w_taps):
                        if 0 <= k - a < n_acc:
                            acc[k - a] = acc[k - a] + w_tap * v
            for n in range(n_acc):
                y_scr[cb, r0 + n * sub:r0 + (n + 1) * sub, :] = acc[n]
        return carry

    lax.fori_loop(0, n_blocks, channel_block, 0)

    y = y_scr[...]
    mu = jnp.sum(jnp.sum(y, axis=0), axis=-1, keepdims=True) * (1.0 / D_MODEL)
    yc = y - mu
    var = jnp.sum(jnp.sum(yc * yc, axis=0), axis=-1, keepdims=True) * (1.0 / D_MODEL)
    z = yc * lax.rsqrt(var + EPS) * lng_ref[...] + lnb_ref[...]
    z = z * _sigmoid(z)
    for cb in range(n_blocks):
        lhs_scr[:, cb * LANES:(cb + 1) * LANES] = z[cb].astype(BF16)
    out = _dot(lhs_scr[...], w2_ref[...]) + b2_ref[...]
    o_ref[...] = _post_residual(out, x_ref[...], mod_ref[0], gpost_ref[...], 2)


def _conv_out(g, w_dw, b_dw, ln_g, ln_b, w2, b2, x, mods, g_post, n_rows):
    tm = 256
    halo_per_tile = tm // HALO
    last_halo = N_TOK // HALO - 1
    n_blocks = D_MODEL // LANES

    def by_block(v):
        return v.reshape(v.shape[0], n_blocks, LANES).swapaxes(0, 1)

    return pl.pallas_call(
        functools.partial(_conv_out_kernel, tm),
        grid=(n_rows // tm,),
        in_specs=[
            _row_spec(tm, D_MODEL),
            pl.BlockSpec((HALO, D_MODEL), lambda i: (jnp.maximum(i * halo_per_tile - 1, 0), 0)),
            pl.BlockSpec((HALO, D_MODEL),
                         lambda i: (jnp.minimum((i + 1) * halo_per_tile, last_halo), 0)),
            _resident((n_blocks, CONV_WIDTH, LANES)),
            _resident((n_blocks, 1, LANES)),
            _resident((n_blocks, 1, LANES)),
            _resident((n_blocks, 1, LANES)),
            _resident((D_MODEL, D_MODEL)),
            _resident((1, D_MODEL)),
            _row_spec(tm, D_MODEL),
            _mod_spec(tm),
            _resident((1, D_MODEL)),
        ],
        out_specs=_row_spec(tm, D_MODEL),
        out_shape=jax.ShapeDtypeStruct((n_rows, D_MODEL), F32),
        scratch_shapes=[pltpu.VMEM((n_blocks, tm + 2 * HALO, LANES), F32),
                        pltpu.VMEM((n_blocks, tm, LANES), F32),
                        pltpu.VMEM((SUBLANES, CONV_SPAN, LANES), F32),
                        pltpu.VMEM((tm, D_MODEL), BF16)],
        compiler_params=_params(1),
        name="conv_dw_out",
    )(g, g, g, by_block(w_dw), by_block(b_dw), by_block(ln_g), by_block(ln_b), w2, b2, x, mods,
      g_post)


def _gmlp_in_kernel(x_ref, mod_ref, gpre_ref, w_ref, b_ref, lng_ref, lnb_ref, u_ref, v_ref):
    h = _premix(x_ref[...], mod_ref[0], gpre_ref[...], 0).astype(BF16)
    u_ref[...] = _gelu(_dot(h, w_ref[:, :D_MODEL]) + b_ref[:, :D_MODEL])
    v = _gelu(_dot(h, w_ref[:, D_MODEL:]) + b_ref[:, D_MODEL:])
    v_ref[...] = _layer_norm(v, lng_ref[...], lnb_ref[...]).astype(BF16)


def _gmlp_in(x, mods, g_pre, w_uv, b_uv, ln_g, ln_b):
    tm = 256
    return pl.pallas_call(
        _gmlp_in_kernel,
        grid=(N_TOK // tm,),
        in_specs=[
            _row_spec(tm, D_MODEL),
            _mod_spec(tm),
            _resident((1, D_MODEL)),
            _resident((D_MODEL, 2 * D_MODEL)),
            _resident((1, 2 * D_MODEL)),
            _resident((1, D_MODEL)),
            _resident((1, D_MODEL)),
        ],
        out_specs=[_row_spec(tm, D_MODEL), _row_spec(tm, D_MODEL)],
        out_shape=[jax.ShapeDtypeStruct((N_TOK, D_MODEL), F32),
                   jax.ShapeDtypeStruct((N_TOK, D_MODEL), BF16)],
        compiler_params=_params(1),
        name="gmlp_in",
    )(x, mods, g_pre, w_uv, b_uv, ln_g, ln_b)


def _gmlp_out_kernel(tm, u_ref, v_ref, ws_ref, bs_ref, w_ref, b_ref, x_ref, mod_ref, gpost_ref,
                     o_ref, t_scr):
    for g in range(N_GROUPS):
        c0 = g * GROUP_DIM
        w_s = ws_ref[g]
        for r0 in range(0, tm, CHUNK):
            sv = _dot(w_s, v_ref[r0:r0 + CHUNK, c0:c0 + GROUP_DIM]) + bs_ref[:, c0:c0 + GROUP_DIM]
            t_scr[r0:r0 + CHUNK, c0:c0 + GROUP_DIM] = (
                u_ref[r0:r0 + CHUNK, c0:c0 + GROUP_DIM] * sv).astype(BF16)
    out = _dot(t_scr[...], w_ref[...]) + b_ref[...]
    o_ref[...] = _post_residual(out, x_ref[...], mod_ref[0], gpost_ref[...], 2)


def _gmlp_out(u, v, w_s, b_s_wide, w, b, x, mods, g_post, n_rows):
    tm = 256
    return pl.pallas_call(
        functools.partial(_gmlp_out_kernel, tm),
        grid=(n_rows // tm,),
        in_specs=[
            _row_spec(tm, D_MODEL),
            _row_spec(tm, D_MODEL),
            _resident((N_GROUPS, CHUNK, CHUNK)),
            _resident((CHUNK, D_MODEL)),
            _resident((D_MODEL, D_MODEL)),
            _resident((1, D_MODEL)),
            _row_spec(tm, D_MODEL),
            _mod_spec(tm),
            _resident((1, D_MODEL)),
        ],
        out_specs=_row_spec(tm, D_MODEL),
        out_shape=jax.ShapeDtypeStruct((n_rows, D_MODEL), F32),
        scratch_shapes=[pltpu.VMEM((tm, D_MODEL), BF16)],
        compiler_params=_params(1),
        name="gmlp_spatial_out",
    )(u, v, w_s, b_s_wide, w, b, x, mods, g_post)


def _row(v):
    return v.reshape(1, -1)


def kernel(x, c, ctx, c_ctx, ada_w, ada_b, norm_mix_pre, norm_mix_post, norm_ffn_pre, norm_ffn_post, mla_w_dq, mla_q_norm, mla_w_uq, mla_w_dkv, mla_kv_norm, mla_w_ukv, mla_w_o, conv_w_pw1, conv_b_pw1, conv_w_dw, conv_b_dw, conv_ln_g, conv_ln_b, conv_w_pw2, conv_b_pw2, gmlp_w_uv, gmlp_b_uv, gmlp_ln_g, gmlp_ln_b, gmlp_w_s, gmlp_b_s, gmlp_w_out, gmlp_b_out, ffn_w1, ffn_w2):
    stream = (x.reshape(N_LAT, D_MODEL), ctx.reshape(N_CTX, D_MODEL))
    cond = jnp.concatenate([c, c_ctx[None, :], jnp.zeros((8 - BATCH - 1, D_MODEL), F32)], axis=0)
    mods_all = _ada_all(cond, ada_w, ada_b)[:, :N_SETS].reshape(DEPTH, N_SETS, N_MOD, D_MODEL)
    cos_tab, sin_tab = _rope_tables()
    zero_bias = jnp.zeros((1, D_MODEL), F32)
    ffn_w1_bf16, ffn_w2_bf16 = ffn_w1.astype(BF16), ffn_w2.astype(BF16)

    for i in range(DEPTH):
        last = i == DEPTH - 1
        kind, j = i % N_MIXERS, i // N_MIXERS
        n_rows = N_LAT if last else N_TOK
        mods = mods_all[i]
        if kind == 0:
            w_dq, w_uq, w_dkv, w_uk_t, w_uv, w_o = _mla_weights(
                mla_w_dq[j], mla_w_uq[j], mla_w_dkv[j], mla_w_ukv[j], mla_w_o[j])
            q, kt, v = _mla_proj(stream, mods, _row(norm_mix_pre[i]), w_dq, _row(mla_q_norm[j]),
                                 w_uq, w_dkv, _row(mla_kv_norm[j]), w_uk_t, w_uv, cos_tab, sin_tab)
            o = _attention(q, kt, v, n_rows)
            if not last:
                o = _attention_ctx(q, kt, v, o)
            stream = _mix_out(o, w_o, zero_bias, stream, mods, _row(norm_mix_post[i]), n_rows)
        elif kind == 1:
            g = _conv_in(stream, mods, _row(norm_mix_pre[i]), conv_w_pw1[j].astype(BF16),
                         _row(conv_b_pw1[j]))
            stream = _conv_out(g, conv_w_dw[j], _row(conv_b_dw[j]), _row(conv_ln_g[j]),
                               _row(conv_ln_b[j]), conv_w_pw2[j].astype(BF16),
                               _row(conv_b_pw2[j]), stream, mods, _row(norm_mix_post[i]), n_rows)
        else:
            u, v = _gmlp_in(stream, mods, _row(norm_mix_pre[i]), gmlp_w_uv[j].astype(BF16),
                            _row(gmlp_b_uv[j]), _row(gmlp_ln_g[j]), _row(gmlp_ln_b[j]))
            b_s_wide = jnp.repeat(gmlp_b_s[j].T, GROUP_DIM, axis=1)
            stream = _gmlp_out(u, v, gmlp_w_s[j].astype(BF16), b_s_wide,
                               gmlp_w_out[j].astype(BF16), _row(gmlp_b_out[j]), stream, mods,
                               _row(norm_mix_post[i]), n_rows)
        stream = _ffn(stream, mods, _row(norm_ffn_pre[i]), _row(norm_ffn_post[i]),
                      ffn_w1_bf16, ffn_w2_bf16, i, n_rows)
    return stream.reshape(BATCH, SEQ, D_MODEL)
```

```python
import functools

import jax
import jax.numpy as jnp
import numpy as np
from jax import lax
from jax.experimental import pallas as pl
from jax.experimental.pallas import tpu as pltpu

D_MODEL = 2048
BATCH = 2
SEQ = 4096
DEPTH = 4
CTX_LEN = 256
GRID_W = 64
N_MIXERS = 3
N_HEADS = 16
QK_NOPE_DIM = 128
QK_ROPE_DIM = 64
QK_DIM = QK_NOPE_DIM + QK_ROPE_DIM
V_HEAD_DIM = 128
Q_LORA_RANK = 768
KV_LORA_RANK = 512
ROPE_THETA = 10000.0
ATTN_SCALE = QK_DIM ** -0.5
CONV_WIDTH = 31
CONV_PAD = CONV_WIDTH // 2
CHUNK = 128
N_GROUPS = 16
GROUP_DIM = D_MODEL // N_GROUPS
D_FF = 4 * D_MODEL
N_MOD = 6
EPS = 1e-6

N_LAT = BATCH * SEQ
N_CTX = BATCH * CTX_LEN
N_TOK = N_LAT + N_CTX
N_SETS = 3
HEAD_PAD = 256
LANES = 128
SUBLANES = 8
HALO = 16
CONV_ROWS = 128
CONV_FIRST = HALO - CONV_PAD
CONV_SPAN = CONV_ROWS + SUBLANES * ((CONV_FIRST + CONV_WIDTH - 1) // SUBLANES)
VMEM_LIMIT = 56 * 1024 * 1024
ROW_CHUNK = 16

F32 = jnp.float32
BF16 = jnp.bfloat16


def _dot(a, b):
    return jnp.dot(a, b, preferred_element_type=F32)


def _rms(x, g):
    return x * lax.rsqrt(jnp.mean(x * x, axis=-1, keepdims=True) + EPS) * g


def _layer_norm(x, g, b):
    mu = jnp.mean(x, axis=-1, keepdims=True)
    xc = x - mu
    var = jnp.mean(xc * xc, axis=-1, keepdims=True)
    return xc * lax.rsqrt(var + EPS) * g + b


def _sigmoid(x):
    return 1.0 / (1.0 + jnp.exp(-x))


def _gelu(x):
    return 0.5 * x * (1.0 + lax.erf(x * np.float32(np.sqrt(0.5))))


def _params(n_grid_axes):
    return pltpu.CompilerParams(
        dimension_semantics=("arbitrary",) * n_grid_axes,
        vmem_limit_bytes=VMEM_LIMIT)


def _resident(shape):
    nd = len(shape)
    return pl.BlockSpec(shape, lambda *_: (0,) * nd, pipeline_mode=pl.Buffered(1))


def _mod_spec(tm):
    return pl.BlockSpec((1, N_MOD, D_MODEL),
                        lambda i, *_: (jnp.minimum(i * tm // SEQ, N_SETS - 1), 0, 0))


def _row_spec(tm, width):
    return pl.BlockSpec((tm, width), lambda i, *_: (i, 0))


def _stream_operands(stream, tm):
    if not isinstance(stream, tuple):
        return (stream,), (_row_spec(tm, D_MODEL),)
    lat_tiles = N_LAT // tm
    return stream, (
        pl.BlockSpec((tm, D_MODEL), lambda i, *_: (jnp.minimum(i, lat_tiles - 1), 0)),
        pl.BlockSpec((tm, D_MODEL), lambda i, *_: (jnp.maximum(i - lat_tiles, 0), 0)))


def _stream_rows(x_refs, r0):
    rows = slice(r0, r0 + ROW_CHUNK)
    if len(x_refs) == 1:
        return x_refs[0][rows, :]
    lat_ref, ctx_ref = x_refs
    return jnp.where(pl.program_id(0) < N_LAT // lat_ref.shape[0], lat_ref[rows, :],
                     ctx_ref[rows, :])


def _premix_to(h_ref, x_refs, mod, g, shift_idx):
    gain = g * (1.0 + mod[shift_idx + 1:shift_idx + 2])
    shift = mod[shift_idx:shift_idx + 1]
    for r0 in range(0, h_ref.shape[0], ROW_CHUNK):
        x = _stream_rows(x_refs, r0)
        inv = lax.rsqrt(jnp.mean(x * x, axis=-1, keepdims=True) + EPS)
        h_ref[r0:r0 + ROW_CHUNK, :] = (x * inv * gain + shift).astype(BF16)


def _post_residual_in_place(o_ref, x_refs, mod, g_post, gate_idx):
    gain = mod[gate_idx:gate_idx + 1] * g_post
    for r0 in range(0, o_ref.shape[0], ROW_CHUNK):
        y = o_ref[r0:r0 + ROW_CHUNK, :]
        inv = lax.rsqrt(jnp.mean(y * y, axis=-1, keepdims=True) + EPS)
        o_ref[r0:r0 + ROW_CHUNK, :] = _stream_rows(x_refs, r0) + y * inv * gain


def _ada_kernel(cond_ref, w_ref, b_ref, o_ref):
    c = cond_ref[...]
    s = (c * _sigmoid(c)).astype(BF16)
    o_ref[0] = _dot(s, w_ref[0].astype(BF16)) + b_ref[0]


def _ada_all(cond, ada_w, ada_b):
    tn = 1024
    n_out = N_MOD * D_MODEL
    return pl.pallas_call(
        _ada_kernel,
        grid=(DEPTH, n_out // tn),
        in_specs=[
            pl.BlockSpec((8, D_MODEL), lambda l, j: (0, 0)),
            pl.BlockSpec((1, D_MODEL, tn), lambda l, j: (l, 0, j)),
            pl.BlockSpec((1, 1, tn), lambda l, j: (l, 0, j)),
        ],
        out_specs=pl.BlockSpec((1, 8, tn), lambda l, j: (l, 0, j)),
        out_shape=jax.ShapeDtypeStruct((DEPTH, 8, n_out), F32),
        compiler_params=_params(2),
        name="ada_mod",
    )(cond, ada_w, ada_b.reshape(DEPTH, 1, n_out))


def _ffn_kernel(n_f, n_sub, x_ref, mod_ref, gpre_ref, gpost_ref, w1_ref, w2_ref, o_ref, h_scr):
    f = pl.program_id(1)
    sub_width = w1_ref.shape[2] // n_sub

    @pl.when(f == 0)
    def _():
        _premix_to(h_scr, (x_ref,), mod_ref[0], gpre_ref[...], 3)
        o_ref[...] = jnp.zeros(o_ref.shape, F32)

    for s in range(n_sub):
        cols = slice(s * sub_width, (s + 1) * sub_width)
        a = _dot(h_scr[...], w1_ref[0, :, cols])
        a = jnp.square(jnp.maximum(a, 0.0)).astype(BF16)
        o_ref[...] += _dot(a, w2_ref[0, cols, :])

    @pl.when(f == n_f - 1)
    def _():
        _post_residual_in_place(o_ref, (x_ref,), mod_ref[0], gpost_ref[...], 5)


def _ffn(x, mods, g_pre, g_post, w1, w2, layer, n_rows):
    tm, tf, n_sub = 512, 2048, 2
    n_f = D_FF // tf
    return pl.pallas_call(
        functools.partial(_ffn_kernel, n_f, n_sub),
        grid=(n_rows // tm, n_f),
        in_specs=[
            _row_spec(tm, D_MODEL),
            _mod_spec(tm),
            pl.BlockSpec((1, D_MODEL), lambda i, f: (0, 0)),
            pl.BlockSpec((1, D_MODEL), lambda i, f: (0, 0)),
            pl.BlockSpec((1, D_MODEL, tf), lambda i, f: (layer, 0, f)),
            pl.BlockSpec((1, tf, D_MODEL), lambda i, f: (layer, f, 0)),
        ],
        out_specs=_row_spec(tm, D_MODEL),
        out_shape=jax.ShapeDtypeStruct((n_rows, D_MODEL), F32),
        scratch_shapes=[pltpu.VMEM((tm, D_MODEL), BF16)],
        compiler_params=_params(2),
        name="ffn",
    )(x, mods, g_pre, g_post, w1, w2)


def _mix_out_kernel(n_x, a_ref, w_ref, b_ref, *refs):
    x_refs, (mod_ref, gpost_ref, o_ref) = refs[:n_x], refs[n_x:]
    o_ref[...] = _dot(a_ref[...], w_ref[...]) + b_ref[...]
    _post_residual_in_place(o_ref, x_refs, mod_ref[0], gpost_ref[...], 2)


def _mix_out(a, w, b, stream, mods, g_post, n_rows):
    tm = 256
    xs, x_specs = _stream_operands(stream, tm)
    return pl.pallas_call(
        functools.partial(_mix_out_kernel, len(xs)),
        grid=(n_rows // tm,),
        in_specs=[
            _row_spec(tm, D_MODEL),
            _resident((D_MODEL, D_MODEL)),
            _resident((1, D_MODEL)),
            *x_specs,
            _mod_spec(tm),
            _resident((1, D_MODEL)),
        ],
        out_specs=_row_spec(tm, D_MODEL),
        out_shape=jax.ShapeDtypeStruct((n_rows, D_MODEL), F32),
        compiler_params=_params(1),
        name="mix_out",
    )(a, w, b, *xs, mods, g_post)


def _rope_tables():
    f32 = np.float32
    t = np.arange(SEQ, dtype=np.int32)
    row = (t // GRID_W).astype(f32)
    col = (t % GRID_W).astype(f32)
    half = QK_ROPE_DIM // 2
    inv = (f32(ROPE_THETA) ** (-np.arange(0, half, 2, dtype=f32) / f32(half))).astype(f32)
    ang_r, ang_c = row[:, None] * inv, col[:, None] * inv
    cr, sr, cc, sc = np.cos(ang_r), np.sin(ang_r), np.cos(ang_c), np.sin(ang_c)
    zeros = np.zeros((SEQ, QK_ROPE_DIM), f32)
    cos_tab = np.concatenate([cr, cr, cc, cc, zeros], axis=1)
    sin_tab = np.concatenate([-sr, sr, -sc, sc, zeros], axis=1)
    ident = np.concatenate([np.ones((CTX_LEN, QK_ROPE_DIM), f32),
                            np.zeros((CTX_LEN, QK_ROPE_DIM), f32)], axis=1)
    cos_tab = np.concatenate([cos_tab, ident], axis=0)
    sin_tab = np.concatenate([sin_tab, np.zeros((CTX_LEN, LANES), f32)], axis=0)
    return jnp.asarray(cos_tab, F32), jnp.asarray(sin_tab, F32)


def _swap_half_lanes(w):
    perm = np.arange(QK_ROPE_DIM) ^ (QK_ROPE_DIM // 4)
    return w[..., perm]


def _mla_weights(w_dq, w_uq, w_dkv, w_ukv, w_o):
    uq = w_uq.reshape(Q_LORA_RANK, N_HEADS, QK_DIM)
    rope = uq[..., QK_NOPE_DIM:]
    uq = jnp.concatenate([uq[..., :QK_NOPE_DIM], rope, _swap_half_lanes(rope)], axis=-1)
    kpe = w_dkv[:, KV_LORA_RANK:]
    dkv = jnp.concatenate([w_dkv[:, :KV_LORA_RANK], kpe, _swap_half_lanes(kpe)], axis=-1)
    ukv = w_ukv.reshape(KV_LORA_RANK, N_HEADS, QK_NOPE_DIM + V_HEAD_DIM)
    uk_t = ukv[..., :QK_NOPE_DIM].reshape(KV_LORA_RANK, N_HEADS * QK_NOPE_DIM).T
    uv = ukv[..., QK_NOPE_DIM:].reshape(KV_LORA_RANK, N_HEADS * V_HEAD_DIM)
    return (w_dq.astype(BF16), uq.reshape(Q_LORA_RANK, N_HEADS * HEAD_PAD).astype(BF16),
            dkv.astype(BF16), uk_t.astype(BF16), uv.astype(BF16), w_o.astype(BF16))


def _rope(blk, cos_tab, sin_tab):
    return blk * cos_tab + pltpu.roll(blk, LANES // 2, axis=1) * sin_tab


def _mla_proj_kernel(n_x, *refs):
    x_refs = refs[:n_x]
    (mod_ref, gpre_ref, wdq_ref, qn_ref, wuq_ref, wdkv_ref, kvn_ref, wukt_ref, wuv_ref, cos_ref,
     sin_ref, q_ref, kt_ref, v_ref, h_scr) = refs[n_x:]
    _premix_to(h_scr, x_refs, mod_ref[0], gpre_ref[...], 0)
    h = h_scr[...]
    cos_tab, sin_tab = cos_ref[...], sin_ref[...]
    q_lat = _rms(_dot(h, wdq_ref[...]), qn_ref[...]).astype(BF16)
    kv_a = _dot(h, wdkv_ref[...])
    c_kv = _rms(kv_a[:, :KV_LORA_RANK], kvn_ref[...]).astype(BF16)
    k_pe_t = _rope(kv_a[:, KV_LORA_RANK:], cos_tab, sin_tab).T.astype(BF16)
    k_nope_t = _qk(wukt_ref[...], c_kv)
    v_ref[...] = _dot(c_kv, wuv_ref[...]).astype(BF16)
    for hd in range(N_HEADS):
        lo = hd * HEAD_PAD
        q = _dot(q_lat, wuq_ref[:, lo:lo + HEAD_PAD])
        q_ref[:, lo:lo + LANES] = (q[:, :LANES] * ATTN_SCALE).astype(BF16)
        q_ref[:, lo + LANES:lo + HEAD_PAD] = (
            _rope(q[:, LANES:], cos_tab, sin_tab) * ATTN_SCALE).astype(BF16)
        kt_ref[lo:lo + LANES, :] = k_nope_t[hd * LANES:(hd + 1) * LANES].astype(BF16)
        kt_ref[lo + LANES:lo + HEAD_PAD, :] = k_pe_t


def _mla_proj(stream, mods, g_pre, w_dq, q_norm, w_uq, w_dkv, kv_norm, w_uk_t, w_uv, cos_tab,
              sin_tab):
    tm = 256
    xs, x_specs = _stream_operands(stream, tm)
    lat_tiles, seq_tiles = N_LAT // tm, SEQ // tm
    tab_spec = pl.BlockSpec(
        (tm, LANES), lambda i: (jnp.where(i < lat_tiles, i % seq_tiles, seq_tiles), 0))
    wide = N_HEADS * HEAD_PAD
    return pl.pallas_call(
        functools.partial(_mla_proj_kernel, len(xs)),
        grid=(N_TOK // tm,),
        in_specs=[
            *x_specs,
            _mod_spec(tm),
            _resident((1, D_MODEL)),
            _resident((D_MODEL, Q_LORA_RANK)),
            _resident((1, Q_LORA_RANK)),
            _resident((Q_LORA_RANK, wide)),
            _resident((D_MODEL, KV_LORA_RANK + LANES)),
            _resident((1, KV_LORA_RANK)),
            _resident((N_HEADS * QK_NOPE_DIM, KV_LORA_RANK)),
            _resident((KV_LORA_RANK, N_HEADS * V_HEAD_DIM)),
            tab_spec,
            tab_spec,
        ],
        out_specs=[_row_spec(tm, wide), pl.BlockSpec((wide, tm), lambda i: (0, i)),
                   _row_spec(tm, D_MODEL)],
        out_shape=[jax.ShapeDtypeStruct((N_TOK, wide), BF16),
                   jax.ShapeDtypeStruct((wide, N_TOK), BF16),
                   jax.ShapeDtypeStruct((N_TOK, D_MODEL), BF16)],
        scratch_shapes=[pltpu.VMEM((tm, D_MODEL), BF16)],
        compiler_params=_params(1),
        name="mla_proj",
    )(*xs, mods, g_pre, w_dq, q_norm, w_uq, w_dkv, kv_norm, w_uk_t, w_uv, cos_tab, sin_tab)


def _qk(q, k):
    return lax.dot_general(q, k, (((1,), (1,)), ((), ())), preferred_element_type=F32)


def _attn_kernel(tq, tk, q_ref, ktc_ref, ktl_ref, vc_ref, vl_ref, qn_ref, ktcn_ref, ktln_ref, o_ref,
                 s_even, s_odd, m_even, m_odd, vaug_scr):
    vaug_scr[0:CTX_LEN, 0:V_HEAD_DIM] = vc_ref[...]
    vaug_scr[CTX_LEN:, 0:V_HEAD_DIM] = vl_ref[...]
    vaug_scr[:, V_HEAD_DIM:] = jnp.ones((CTX_LEN + SEQ, LANES), BF16)

    chunks = [(0, CTX_LEN)] + [(CTX_LEN + c * tk, tk) for c in range(SEQ // tk)]
    n_tiles = SEQ // tq

    def rows(tile):
        start = tile * tq if isinstance(tile, int) else pl.multiple_of(tile * tq, tq)
        return pl.ds(start, tq)

    def scores(q, kc_ref, kl_ref, s_scr, m_scr):
        for n, (off, width) in enumerate(chunks):
            kt = kc_ref[...] if n == 0 else kl_ref[:, off - CTX_LEN:off - CTX_LEN + width]
            s = _dot(q, kt)
            s_scr[:, off:off + width] = s
            cm = s[:, 0:LANES]
            for t in range(1, width // LANES):
                cm = jnp.maximum(cm, s[:, t * LANES:(t + 1) * LANES])
            m_scr[...] = cm if n == 0 else jnp.maximum(m_scr[...], cm)

    def outputs(tile, s_scr, m_scr):
        m = jnp.max(m_scr[...], axis=1, keepdims=True)
        acc = None
        for off, width in chunks:
            p = jnp.exp(s_scr[:, off:off + width] - m).astype(BF16)
            part = _dot(p, vaug_scr[off:off + width, :])
            acc = part if acc is None else acc + part
        o_ref[rows(tile), :] = (acc[:, :V_HEAD_DIM] / acc[:, V_HEAD_DIM:]).astype(BF16)

    def own_scores(tile, s_scr, m_scr):
        scores(q_ref[rows(tile), :], ktc_ref, ktl_ref, s_scr, m_scr)

    @pl.when(pl.program_id(0) == 0)
    def _():
        own_scores(0, s_even, m_even)

    def tile_pair(j, carry):
        t = 2 * j
        own_scores(t + 1, s_odd, m_odd)
        outputs(t, s_even, m_even)
        own_scores(t + 2, s_even, m_even)
        outputs(t + 1, s_odd, m_odd)
        return carry

    lax.fori_loop(0, n_tiles // 2 - 1, tile_pair, 0)
    own_scores(n_tiles - 1, s_odd, m_odd)
    outputs(n_tiles - 2, s_even, m_even)
    scores(qn_ref[...], ktcn_ref, ktln_ref, s_even, m_even)
    outputs(n_tiles - 1, s_odd, m_odd)


def _attention(q, kt, v, n_out_rows):
    tq, tk = 512, 256
    ctx_blk = N_LAT // CTX_LEN
    n_keys = CTX_LEN + SEQ
    n_steps = BATCH * N_HEADS

    def this(s):
        return s // N_HEADS, s % N_HEADS

    def following(s):
        return this(jnp.minimum(s + 1, n_steps - 1))

    def spec(shape, pick, index):
        return pl.BlockSpec(shape, lambda s: index(*pick(s)))

    return pl.pallas_call(
        functools.partial(_attn_kernel, tq, tk),
        grid=(n_steps,),
        in_specs=[
            spec((SEQ, HEAD_PAD), this, lambda b, h: (b, h)),
            spec((HEAD_PAD, CTX_LEN), this, lambda b, h: (h, ctx_blk + b)),
            spec((HEAD_PAD, SEQ), this, lambda b, h: (h, b)),
            spec((CTX_LEN, V_HEAD_DIM), this, lambda b, h: (ctx_blk + b, h)),
            spec((SEQ, V_HEAD_DIM), this, lambda b, h: (b, h)),
            spec((tq, HEAD_PAD), following, lambda b, h: (b * (SEQ // tq), h)),
            spec((HEAD_PAD, CTX_LEN), following, lambda b, h: (h, ctx_blk + b)),
            spec((HEAD_PAD, SEQ), following, lambda b, h: (h, b)),
        ],
        out_specs=spec((SEQ, V_HEAD_DIM), this, lambda b, h: (b, h)),
        out_shape=jax.ShapeDtypeStruct((n_out_rows, N_HEADS * V_HEAD_DIM), BF16),
        scratch_shapes=[pltpu.VMEM((tq, n_keys), F32), pltpu.VMEM((tq, n_keys), F32),
                        pltpu.VMEM((tq, LANES), F32), pltpu.VMEM((tq, LANES), F32),
                        pltpu.VMEM((n_keys, V_HEAD_DIM + LANES), BF16)],
        compiler_params=_params(1),
        name="mla_attention",
    )(q, kt, kt, v, v, q, kt, kt)


def _attn_ctx_kernel(q_ref, kt_ref, v_ref, o_in_ref, o_ref):
    del o_in_ref
    s = _dot(q_ref[...], kt_ref[...])
    p = jnp.exp(s - jnp.max(s, axis=1, keepdims=True))
    l = jnp.sum(p, axis=1, keepdims=True)
    o_ref[...] = (_dot(p.astype(BF16), v_ref[...]) / l).astype(BF16)


def _attention_ctx(q, kt, v, o):
    ctx_blk = N_LAT // CTX_LEN
    q_spec = pl.BlockSpec((CTX_LEN, HEAD_PAD), lambda b, h: (ctx_blk + b, h))
    kt_spec = pl.BlockSpec((HEAD_PAD, CTX_LEN), lambda b, h: (h, ctx_blk + b))
    v_spec = pl.BlockSpec((CTX_LEN, V_HEAD_DIM), lambda b, h: (ctx_blk + b, h))
    return pl.pallas_call(
        _attn_ctx_kernel,
        grid=(BATCH, N_HEADS),
        in_specs=[q_spec, kt_spec, v_spec, pl.BlockSpec(memory_space=pl.ANY)],
        out_specs=v_spec,
        out_shape=jax.ShapeDtypeStruct(o.shape, o.dtype),
        input_output_aliases={3: 0},
        compiler_params=_params(2),
        name="mla_attention_ctx",
    )(q, kt, v, o)


def _conv_in_kernel(x_ref, mod_ref, gpre_ref, w_ref, b_ref, o_ref, h_scr):
    _premix_to(h_scr, (x_ref,), mod_ref[0], gpre_ref[...], 0)
    h = h_scr[...]
    a = _dot(h, w_ref[:, :D_MODEL]) + b_ref[:, :D_MODEL]
    gate = _dot(h, w_ref[:, D_MODEL:]) + b_ref[:, D_MODEL:]
    o_ref[...] = a * _sigmoid(gate)


def _conv_in(x, mods, g_pre, w, b):
    tm = 256
    return pl.pallas_call(
        _conv_in_kernel,
        grid=(N_TOK // tm,),
        in_specs=[
            _row_spec(tm, D_MODEL),
            _mod_spec(tm),
            _resident((1, D_MODEL)),
            _resident((D_MODEL, 2 * D_MODEL)),
            _resident((1, 2 * D_MODEL)),
        ],
        out_specs=_row_spec(tm, D_MODEL),
        out_shape=jax.ShapeDtypeStruct((N_TOK, D_MODEL), F32),
        scratch_shapes=[pltpu.VMEM((tm, D_MODEL), BF16)],
        compiler_params=_params(1),
        name="conv_in_glu",
    )(x, mods, g_pre, w, b)


def _conv_out_kernel(tm, g_ref, left_ref, right_ref, wdw_ref, bdw_ref, lng_ref, lnb_ref, w2_ref,
                     b2_ref, x_ref, mod_ref, gpost_ref, o_ref, pad_scr, y_scr, shift_scr, lhs_scr):
    i = pl.program_id(0)
    seg_tiles = SEQ // tm
    in_lat = i < N_LAT // tm
    has_left = jnp.logical_and(in_lat, i % seg_tiles != 0)
    has_right = jnp.logical_and(in_lat, i % seg_tiles != seg_tiles - 1)
    n_blocks = D_MODEL // LANES
    for cb in range(n_blocks):
        lanes = slice(cb * LANES, (cb + 1) * LANES)
        pad_scr[cb, 0:HALO, :] = jnp.where(has_left, left_ref[:, lanes], 0.0)
        pad_scr[cb, HALO:HALO + tm, :] = g_ref[:, lanes]
        pad_scr[cb, HALO + tm:HALO + tm + HALO, :] = jnp.where(has_right, right_ref[:, lanes], 0.0)

    rows, sub, first, span = CONV_ROWS, SUBLANES, CONV_FIRST, CONV_SPAN
    n_acc = rows // sub

    def channel_block(cb, carry):
        for r0 in range(0, tm, rows):
            acc = [jnp.broadcast_to(bdw_ref[cb], (sub, LANES))] * n_acc
            for res in range(sub):
                if res:
                    shift_scr[res] = pad_scr[cb, r0 + res:r0 + res + span, :]
                taps = [(off // sub, off - first) for off in range(res, first + CONV_WIDTH, sub)
                        if off >= first]
                w_taps = [jnp.broadcast_to(wdw_ref[cb, j:j + 1, :], (sub, LANES)) for _, j in taps]
                for k in range(span // sub):
                    lo = k * sub
                    v = (shift_scr[res, lo:lo + sub, :] if res
                         else pad_scr[cb, r0 + lo:r0 + lo + sub, :])
                    for (a, _), w_tap in zip(taps, w_taps):
                        if 0 <= k - a < n_acc:
                            acc[k - a] = acc[k - a] + w_tap * v
            for n in range(n_acc):
                y_scr[cb, r0 + n * sub:r0 + (n + 1) * sub, :] = acc[n]
        return carry

    lax.fori_loop(0, n_blocks, channel_block, 0)

    y = y_scr[...]
    mu = jnp.sum(jnp.sum(y, axis=0), axis=-1, keepdims=True) * (1.0 / D_MODEL)
    yc = y - mu
    var = jnp.sum(jnp.sum(yc * yc, axis=0), axis=-1, keepdims=True) * (1.0 / D_MODEL)
    z = yc * lax.rsqrt(var + EPS) * lng_ref[...] + lnb_ref[...]
    z = z * _sigmoid(z)
    for cb in range(n_blocks):
        lhs_scr[:, cb * LANES:(cb + 1) * LANES] = z[cb].astype(BF16)
    o_ref[...] = _dot(lhs_scr[...], w2_ref[...]) + b2_ref[...]
    _post_residual_in_place(o_ref, (x_ref,), mod_ref[0], gpost_ref[...], 2)


def _conv_out(g, w_dw, b_dw, ln_g, ln_b, w2, b2, x, mods, g_post, n_rows):
    tm = 256
    halo_per_tile = tm // HALO
    last_halo = N_TOK // HALO - 1
    n_blocks = D_MODEL // LANES

    def by_block(v):
        return v.reshape(v.shape[0], n_blocks, LANES).swapaxes(0, 1)

    return pl.pallas_call(
        functools.partial(_conv_out_kernel, tm),
        grid=(n_rows // tm,),
        in_specs=[
            _row_spec(tm, D_MODEL),
            pl.BlockSpec((HALO, D_MODEL), lambda i: (jnp.maximum(i * halo_per_tile - 1, 0), 0)),
            pl.BlockSpec((HALO, D_MODEL),
                         lambda i: (jnp.minimum((i + 1) * halo_per_tile, last_halo), 0)),
            _resident((n_blocks, CONV_WIDTH, LANES)),
            _resident((n_blocks, 1, LANES)),
            _resident((n_blocks, 1, LANES)),
            _resident((n_blocks, 1, LANES)),
            _resident((D_MODEL, D_MODEL)),
            _resident((1, D_MODEL)),
            _row_spec(tm, D_MODEL),
            _mod_spec(tm),
            _resident((1, D_MODEL)),
        ],
        out_specs=_row_spec(tm, D_MODEL),
        out_shape=jax.ShapeDtypeStruct((n_rows, D_MODEL), F32),
        scratch_shapes=[pltpu.VMEM((n_blocks, tm + 2 * HALO, LANES), F32),
                        pltpu.VMEM((n_blocks, tm, LANES), F32),
                        pltpu.VMEM((SUBLANES, CONV_SPAN, LANES), F32),
                        pltpu.VMEM((tm, D_MODEL), BF16)],
        compiler_params=_params(1),
        name="conv_dw_out",
    )(g, g, g, by_block(w_dw), by_block(b_dw), by_block(ln_g), by_block(ln_b), w2, b2, x, mods,
      g_post)


def _gmlp_in_kernel(x_ref, mod_ref, gpre_ref, w_ref, b_ref, lng_ref, lnb_ref, u_ref, v_ref, h_scr):
    _premix_to(h_scr, (x_ref,), mod_ref[0], gpre_ref[...], 0)
    h = h_scr[...]
    u_ref[...] = _gelu(_dot(h, w_ref[:, :D_MODEL]) + b_ref[:, :D_MODEL])
    v = _gelu(_dot(h, w_ref[:, D_MODEL:]) + b_ref[:, D_MODEL:])
    v_ref[...] = _layer_norm(v, lng_ref[...], lnb_ref[...]).astype(BF16)


def _gmlp_in(x, mods, g_pre, w_uv, b_uv, ln_g, ln_b):
    tm = 256
    return pl.pallas_call(
        _gmlp_in_kernel,
        grid=(N_TOK // tm,),
        in_specs=[
            _row_spec(tm, D_MODEL),
            _mod_spec(tm),
            _resident((1, D_MODEL)),
            _resident((D_MODEL, 2 * D_MODEL)),
            _resident((1, 2 * D_MODEL)),
            _resident((1, D_MODEL)),
            _resident((1, D_MODEL)),
        ],
        out_specs=[_row_spec(tm, D_MODEL), _row_spec(tm, D_MODEL)],
        out_shape=[jax.ShapeDtypeStruct((N_TOK, D_MODEL), F32),
                   jax.ShapeDtypeStruct((N_TOK, D_MODEL), BF16)],
        scratch_shapes=[pltpu.VMEM((tm, D_MODEL), BF16)],
        compiler_params=_params(1),
        name="gmlp_in",
    )(x, mods, g_pre, w_uv, b_uv, ln_g, ln_b)


def _gmlp_out_kernel(tm, u_ref, v_ref, ws_ref, bs_ref, w_ref, b_ref, x_ref, mod_ref, gpost_ref,
                     o_ref, t_scr):
    for g in range(N_GROUPS):
        c0 = g * GROUP_DIM
        w_s = ws_ref[g]
        for r0 in range(0, tm, CHUNK):
            sv = _dot(w_s, v_ref[r0:r0 + CHUNK, c0:c0 + GROUP_DIM]) + bs_ref[:, c0:c0 + GROUP_DIM]
            t_scr[r0:r0 + CHUNK, c0:c0 + GROUP_DIM] = (
                u_ref[r0:r0 + CHUNK, c0:c0 + GROUP_DIM] * sv).astype(BF16)
    o_ref[...] = _dot(t_scr[...], w_ref[...]) + b_ref[...]
    _post_residual_in_place(o_ref, (x_ref,), mod_ref[0], gpost_ref[...], 2)


def _gmlp_out(u, v, w_s, b_s_wide, w, b, x, mods, g_post, n_rows):
    tm = 256
    return pl.pallas_call(
        functools.partial(_gmlp_out_kernel, tm),
        grid=(n_rows // tm,),
        in_specs=[
            _row_spec(tm, D_MODEL),
            _row_spec(tm, D_MODEL),
            _resident((N_GROUPS, CHUNK, CHUNK)),
            _resident((CHUNK, D_MODEL)),
            _resident((D_MODEL, D_MODEL)),
            _resident((1, D_MODEL)),
            _row_spec(tm, D_MODEL),
            _mod_spec(tm),
            _resident((1, D_MODEL)),
        ],
        out_specs=_row_spec(tm, D_MODEL),
        out_shape=jax.ShapeDtypeStruct((n_rows, D_MODEL), F32),
        scratch_shapes=[pltpu.VMEM((tm, D_MODEL), BF16)],
        compiler_params=_params(1),
        name="gmlp_spatial_out",
    )(u, v, w_s, b_s_wide, w, b, x, mods, g_post)


def _row(v):
    return v.reshape(1, -1)


def kernel(x, c, ctx, c_ctx, ada_w, ada_b, norm_mix_pre, norm_mix_post, norm_ffn_pre, norm_ffn_post, mla_w_dq, mla_q_norm, mla_w_uq, mla_w_dkv, mla_kv_norm, mla_w_ukv, mla_w_o, conv_w_pw1, conv_b_pw1, conv_w_dw, conv_b_dw, conv_ln_g, conv_ln_b, conv_w_pw2, conv_b_pw2, gmlp_w_uv, gmlp_b_uv, gmlp_ln_g, gmlp_ln_b, gmlp_w_s, gmlp_b_s, gmlp_w_out, gmlp_b_out, ffn_w1, ffn_w2):
    stream = (x.reshape(N_LAT, D_MODEL), ctx.reshape(N_CTX, D_MODEL))
    cond = jnp.concatenate([c, c_ctx[None, :], jnp.zeros((8 - BATCH - 1, D_MODEL), F32)], axis=0)
    mods_all = _ada_all(cond, ada_w, ada_b)[:, :N_SETS].reshape(DEPTH, N_SETS, N_MOD, D_MODEL)
    cos_tab, sin_tab = _rope_tables()
    zero_bias = jnp.zeros((1, D_MODEL), F32)
    ffn_w1_bf16, ffn_w2_bf16 = ffn_w1.astype(BF16), ffn_w2.astype(BF16)

    for i in range(DEPTH):
        last = i == DEPTH - 1
        kind, j = i % N_MIXERS, i // N_MIXERS
        n_rows = N_LAT if last else N_TOK
        mods = mods_all[i]
        if kind == 0:
            w_dq, w_uq, w_dkv, w_uk_t, w_uv, w_o = _mla_weights(
                mla_w_dq[j], mla_w_uq[j], mla_w_dkv[j], mla_w_ukv[j], mla_w_o[j])
            q, kt, v = _mla_proj(stream, mods, _row(norm_mix_pre[i]), w_dq, _row(mla_q_norm[j]),
                                 w_uq, w_dkv, _row(mla_kv_norm[j]), w_uk_t, w_uv, cos_tab, sin_tab)
            o = _attention(q, kt, v, n_rows)
            if not last:
                o = _attention_ctx(q, kt, v, o)
            stream = _mix_out(o, w_o, zero_bias, stream, mods, _row(norm_mix_post[i]), n_rows)
        elif kind == 1:
            g = _conv_in(stream, mods, _row(norm_mix_pre[i]), conv_w_pw1[j].astype(BF16),
                         _row(conv_b_pw1[j]))
            stream = _conv_out(g, conv_w_dw[j], _row(conv_b_dw[j]), _row(conv_ln_g[j]),
                               _row(conv_ln_b[j]), conv_w_pw2[j].astype(BF16),
                               _row(conv_b_pw2[j]), stream, mods, _row(norm_mix_post[i]), n_rows)
        else:
            u, v = _gmlp_in(stream, mods, _row(norm_mix_pre[i]), gmlp_w_uv[j].astype(BF16),
                            _row(gmlp_b_uv[j]), _row(gmlp_ln_g[j]), _row(gmlp_ln_b[j]))
            b_s_wide = jnp.repeat(gmlp_b_s[j].T, GROUP_DIM, axis=1)
            stream = _gmlp_out(u, v, gmlp_w_s[j].astype(BF16), b_s_wide,
                               gmlp_w_out[j].astype(BF16), _row(gmlp_b_out[j]), stream, mods,
                               _row(norm_mix_post[i]), n_rows)
        stream = _ffn(stream, mods, _row(norm_ffn_pre[i]), _row(norm_ffn_post[i]),
                      ffn_w1_bf16, ffn_w2_bf16, i, n_rows)
    return stream.reshape(BATCH, SEQ, D_MODEL)
```
